```python
import math
import jax, jax.numpy as jnp
from jax import lax
import numpy as np

D_MODEL = 1024
BATCH = 16
SEQ = 4096
DEPTH = 4

GRID_W = 64
CTX_LEN = 256
HEAD_DIM = 64
A_HEADS = D_MODEL // 128
A_KV_HEADS = A_HEADS // 4
A_GROUP = A_HEADS // A_KV_HEADS
B_HEADS = D_MODEL // 256
B_VDIM = 2 * HEAD_DIM
A_WIDTH = A_HEADS * HEAD_DIM
B_WIDTH = B_HEADS * B_VDIM
D_MIX = A_WIDTH + B_WIDTH
A_Q = A_HEADS * HEAD_DIM
A_KV = A_KV_HEADS * HEAD_DIM
B_QK = B_HEADS * 2 * HEAD_DIM
B_V = B_HEADS * B_VDIM
COL_SIZES = (A_Q, A_KV, A_KV, B_QK, B_QK, B_V)
D_IN = sum(COL_SIZES)
D_FF = ((8 * D_MODEL // 3 + 127) // 128) * 128
CONV_W = 3
Q_BLOCK = 128
ROPE_THETA = 10000.0
EPS = 1e-6

kernel_name = "hybrid_gqa_diffattn_convffn_dit"


def rmsnorm(x, g):
    xf = x.astype(jnp.float32)
    y = xf * lax.rsqrt(jnp.mean(xf * xf, axis=-1, keepdims=True) + EPS)
    return y.astype(x.dtype) * g


def modulate(h, shift, scale):
    return h * (1 + scale) + shift


def axial_rope_tables(rows, dtype):
    row = jnp.repeat(jnp.arange(rows, dtype=jnp.float32), GRID_W)
    col = jnp.tile(jnp.arange(GRID_W, dtype=jnp.float32), rows)
    quarter = HEAD_DIM // 4
    inv_freq = ROPE_THETA ** (-jnp.arange(quarter, dtype=jnp.float32) / quarter)
    ang_r = row[:, None] * inv_freq[None, :]
    ang_c = col[:, None] * inv_freq[None, :]
    cos = jnp.concatenate([jnp.cos(ang_r), jnp.cos(ang_r), jnp.cos(ang_c), jnp.cos(ang_c)], axis=-1)
    sin = jnp.concatenate([jnp.sin(ang_r), jnp.sin(ang_r), jnp.sin(ang_c), jnp.sin(ang_c)], axis=-1)
    return cos.astype(dtype), sin.astype(dtype)


def apply_rope(x, cos, sin):
    x1, x2, x3, x4 = jnp.split(x, 4, axis=-1)
    rot = jnp.concatenate([-x2, x1, -x4, x3], axis=-1)
    return x * cos + rot * sin


def split_cols(p):
    idx = np.cumsum(COL_SIZES)[:-1].tolist()
    return jnp.split(p, idx, axis=-1)


def gqa_heads(aq, ak, av, g_q, g_k):
    b, n, _ = aq.shape
    q = rmsnorm(aq.reshape(b, n, A_KV_HEADS, A_GROUP, HEAD_DIM), g_q).transpose(0, 2, 3, 1, 4)
    k = rmsnorm(ak.reshape(b, n, A_KV_HEADS, HEAD_DIM), g_k).transpose(0, 2, 1, 3)
    v = av.reshape(b, n, A_KV_HEADS, HEAD_DIM).transpose(0, 2, 1, 3)
    return q, k, v


def diff_heads(bq, bk, bv):
    b, n, _ = bq.shape
    q = bq.reshape(b, n, B_HEADS, 2, HEAD_DIM).transpose(0, 2, 3, 1, 4)
    k = bk.reshape(b, n, B_HEADS, 2, HEAD_DIM).transpose(0, 2, 3, 1, 4)
    v = bv.reshape(b, n, B_HEADS, B_VDIM).transpose(0, 2, 1, 3)
    return q, k, v


def gqa_attend(q, k, v):
    s = jnp.einsum('bkgqd,bknd->bkgqn', q, k) * (HEAD_DIM ** -0.5)
    p = jax.nn.softmax(s.astype(jnp.float32), axis=-1).astype(v.dtype)
    return jnp.einsum('bkgqn,bknd->bkgqd', p, v)


def diff_attend(q, k, v, lam):
    s = jnp.einsum('bhjqd,bhjnd->bhjqn', q, k) * (HEAD_DIM ** -0.5)
    p = jax.nn.softmax(s.astype(jnp.float32), axis=-1)
    a = (p[:, :, 0] - lam * p[:, :, 1]).astype(v.dtype)
    return jnp.einsum('bhqn,bhnv->bhqv', a, v)


def over_query_blocks(fn, q, q_axis, out_axis):
    n = q.shape[q_axis]
    nb = n // Q_BLOCK
    qb = q.reshape(q.shape[:q_axis] + (nb, Q_BLOCK) + q.shape[q_axis + 1:])
    qb = jnp.moveaxis(qb, q_axis, 0)
    ob = lax.map(fn, qb)
    ob = jnp.moveaxis(ob, 0, out_axis)
    return ob.reshape(ob.shape[:out_axis] + (n,) + ob.shape[out_axis + 2:])


def merge_heads(a_out, b_out, g_sub, lam_init):
    b, _, _, n, _ = a_out.shape
    a = a_out.transpose(0, 3, 1, 2, 4).reshape(b, n, A_WIDTH)
    bo = rmsnorm(b_out, g_sub) * (1.0 - lam_init)
    bo = bo.transpose(0, 2, 1, 3).reshape(b, n, B_WIDTH)
    return jnp.concatenate([a, bo], axis=-1)


def conv_ffn(h, w_up, conv_w, conv_b, w_down):
    u = h @ w_up
    u = lax.conv_general_dilated(u, conv_w[:, None, :], window_strides=(1,),
                                 padding=((CONV_W // 2, CONV_W // 2),),
                                 dimension_numbers=('NWC', 'WIO', 'NWC'),
                                 feature_group_count=u.shape[-1]) + conv_b
    g, v = jnp.split(u, 2, axis=-1)
    return (jax.nn.silu(g) * v) @ w_down


def setup_inputs(seed: int = 0) -> dict:
    key = jax.random.key(seed)
    ks = jax.random.split(key, 24)
    f32 = jnp.float32
    nrm = lambda k, shape, s: jax.random.normal(k, shape, f32) * s
    return {
        "x": nrm(ks[0], (BATCH, SEQ, D_MODEL), 1.0),
        "c": nrm(ks[1], (BATCH, D_MODEL), 1.0),
        "ctx": nrm(ks[2], (BATCH, CTX_LEN, D_MODEL), 1.0),
        "c_ctx": nrm(ks[3], (D_MODEL,), 1.0),
        "w_ada": nrm(ks[4], (DEPTH, D_MODEL, 6 * D_MODEL), 0.3 * D_MODEL ** -0.5),
        "b_ada": nrm(ks[5], (DEPTH, 6 * D_MODEL), 0.02),
        "g_attn": 1.0 + nrm(ks[6], (DEPTH, D_MODEL), 0.02),
        "w_in": nrm(ks[7], (DEPTH, D_MODEL, D_IN), D_MODEL ** -0.5),
        "g_qa": 1.0 + nrm(ks[8], (DEPTH, HEAD_DIM), 0.02),
        "g_ka": 1.0 + nrm(ks[9], (DEPTH, HEAD_DIM), 0.02),
        "lam_q1": nrm(ks[10], (DEPTH, HEAD_DIM), 0.1),
        "lam_k1": nrm(ks[11], (DEPTH, HEAD_DIM), 0.1),
        "lam_q2": nrm(ks[12], (DEPTH, HEAD_DIM), 0.1),
        "lam_k2": nrm(ks[13], (DEPTH, HEAD_DIM), 0.1),
        "g_sub": 1.0 + nrm(ks[14], (DEPTH, B_VDIM), 0.02),
        "w_o": nrm(ks[15], (DEPTH, D_MIX, D_MODEL), D_MIX ** -0.5),
        "g_ffn": 1.0 + nrm(ks[16], (DEPTH, D_MODEL), 0.02),
        "w_up": nrm(ks[17], (DEPTH, D_MODEL, 2 * D_FF), D_MODEL ** -0.5),
        "conv_w": nrm(ks[18], (DEPTH, CONV_W, 2 * D_FF), CONV_W ** -0.5),
        "conv_b": nrm(ks[19], (DEPTH, 2 * D_FF), 0.02),
        "w_down": nrm(ks[20], (DEPTH, D_FF, D_MODEL), D_FF ** -0.5),
        "g_final": 1.0 + nrm(ks[21], (D_MODEL,), 0.02),
    }


def reference(x, c, ctx, c_ctx, w_ada, b_ada, g_attn, w_in, g_qa, g_ka,
              lam_q1, lam_k1, lam_q2, lam_k2, g_sub, w_o, g_ffn,
              w_up, conv_w, conv_b, w_down, g_final):
    n = x.shape[1]
    rows = n // GRID_W
    cos, sin = axial_rope_tables(rows, x.dtype)
    f32 = jnp.float32
    x_lat, x_ctx = x, ctx
    for l in range(DEPTH):
        last = l == DEPTH - 1
        lam_init = 0.8 - 0.6 * math.exp(-0.3 * l)
        mod_x = (jax.nn.silu(c) @ w_ada[l] + b_ada[l])[:, None, :]
        mod_c = (jax.nn.silu(c_ctx) @ w_ada[l] + b_ada[l])[None, None, :]
        sh1x, sc1x, g1x, sh2x, sc2x, g2x = jnp.split(mod_x, 6, axis=-1)
        sh1c, sc1c, g1c, sh2c, sc2c, g2c = jnp.split(mod_c, 6, axis=-1)

        hx = modulate(rmsnorm(x_lat, g_attn[l]), sh1x, sc1x)
        hc = modulate(rmsnorm(x_ctx, g_attn[l]), sh1c, sc1c)
        aqx, akx, avx, bqx, bkx, bvx = split_cols(hx @ w_in[l])
        aqc, akc, avc, bqc, bkc, bvc = split_cols(hc @ w_in[l])

        qx, kx, vx = gqa_heads(aqx, akx, avx, g_qa[l], g_ka[l])
        qx, kx = apply_rope(qx, cos, sin), apply_rope(kx, cos, sin)
        qc, kc, vc = gqa_heads(aqc, akc, avc, g_qa[l], g_ka[l])
        k_all = jnp.concatenate([kc, kx], axis=2)
        v_all = jnp.concatenate([vc, vx], axis=2)
        a_out_x = over_query_blocks(lambda qb: gqa_attend(qb, k_all, v_all), qx, 3, 3)

        lam = (jnp.exp(jnp.sum(lam_q1[l].astype(f32) * lam_k1[l].astype(f32)))
               - jnp.exp(jnp.sum(lam_q2[l].astype(f32) * lam_k2[l].astype(f32))) + lam_init)
        dqx, dkx, dvx = diff_heads(bqx, bkx, bvx)
        dqx, dkx = apply_rope(dqx, cos, sin), apply_rope(dkx, cos, sin)
        dqc, dkc, dvc = diff_heads(bqc, bkc, bvc)
        dk_all = jnp.concatenate([dkc, dkx], axis=3)
        dv_all = jnp.concatenate([dvc, dvx], axis=2)
        b_out_x = over_query_blocks(lambda qb: diff_attend(qb, dk_all, dv_all, lam), dqx, 3, 2)

        x_lat = x_lat + g1x * (merge_heads(a_out_x, b_out_x, g_sub[l], lam_init) @ w_o[l])
        if not last:
            a_out_c = gqa_attend(qc, kc, vc)
            b_out_c = diff_attend(dqc, dkc, dvc, lam)
            x_ctx = x_ctx + g1c * (merge_heads(a_out_c, b_out_c, g_sub[l], lam_init) @ w_o[l])

        hx = modulate(rmsnorm(x_lat, g_ffn[l]), sh2x, sc2x)
        x_lat = x_lat + g2x * conv_ffn(hx, w_up[l], conv_w[l], conv_b[l], w_down[l])
        if not last:
            hc = modulate(rmsnorm(x_ctx, g_ffn[l]), sh2c, sc2c)
            x_ctx = x_ctx + g2c * conv_ffn(hc, w_up[l], conv_w[l], conv_b[l], w_down[l])

    return rmsnorm(x_lat, g_final)
```

```python
import functools
import math

import jax
import jax.numpy as jnp
import numpy as np
from jax import lax
from jax.experimental import pallas as pl
from jax.experimental.pallas import tpu as pltpu

F32 = jnp.float32
BF16 = jnp.bfloat16

HEAD_DIM = 64
LANES = 128
GRID_W = 64
ROPE_THETA = 10000.0
EPS = 1e-6
CONV_W = 3

ROW_TILE = 256
KV_TILE = 512
FFN_CHUNKS = 2
HALO = 8
VMEM_LIMIT = 56 * 1024 * 1024


def _cparams():
    return pltpu.CompilerParams(
        dimension_semantics=("parallel", "arbitrary"),
        vmem_limit_bytes=VMEM_LIMIT,
    )


def _rms(x):
    return x * lax.rsqrt(jnp.mean(x * x, axis=-1, keepdims=True) + EPS)


def _ada_kernel(c_ref, w_ref, b_ref, o_ref):
    c = c_ref[...]
    s = c * jax.nn.sigmoid(c)
    o_ref[...] = jnp.dot(s.astype(BF16), w_ref[...].astype(BF16),
                         preferred_element_type=F32) + b_ref[...]


def _ada_call(c_all, w_ada, b_ada):
    depth, d, d6 = w_ada.shape
    rows = c_all.shape[0]
    tn = d6 // 4
    return pl.pallas_call(
        _ada_kernel,
        out_shape=jax.ShapeDtypeStruct((depth, rows, d6), F32),
        grid=(depth, d6 // tn),
        in_specs=[
            pl.BlockSpec((rows, d), lambda l, j: (0, 0)),
            pl.BlockSpec((None, d, tn), lambda l, j: (l, 0, j)),
            pl.BlockSpec((None, 1, tn), lambda l, j: (l, 0, j)),
        ],
        out_specs=pl.BlockSpec((None, rows, tn), lambda l, j: (l, 0, j)),
        compiler_params=_cparams(),
        name="adaln",
    )(c_all, w_ada, b_ada.reshape(depth, 1, d6))


def _qkv_kernel(x_ref, mod_ref, g_ref, w_ref, gq_ref, gk_ref, cos_ref, sa_ref, sb_ref, pm_ref,
                qa_ref, qb_ref, ka_ref, va_ref, kb_ref, vb_ref, *, a_q, a_kv, b_qk, b_v):
    t = x_ref.shape[0]
    x = x_ref[...]
    h = (_rms(x) * g_ref[...]) * (1.0 + mod_ref[1:2, :]) + mod_ref[0:1, :]
    p = jnp.dot(h.astype(BF16), w_ref[...], preferred_element_type=F32)

    cos, sin_a, sin_b = cos_ref[...], sa_ref[...], sb_ref[...]
    pm = pm_ref[...]
    lo = lax.broadcasted_iota(jnp.int32, (t, LANES), 1) < HEAD_DIM
    q_scale = HEAD_DIM ** -0.5

    def headnorm(z, g):
        sq = z * z
        hi = sq.astype(BF16)
        lo_part = (sq - hi.astype(F32)).astype(BF16)
        ms = (jnp.dot(hi, pm, preferred_element_type=F32)
              + jnp.dot(lo_part, pm, preferred_element_type=F32))
        return z * lax.rsqrt(ms + EPS) * g

    def rope(z):
        return (z * cos + pltpu.roll(z, HEAD_DIM // 4, 1) * sin_a
                + pltpu.roll(z, LANES - HEAD_DIM // 4, 1) * sin_b)

    def split_halves(z, ref, unit, half_rows):
        ref[unit, 0:half_rows, :] = jnp.where(lo, z, 0.0).astype(BF16)
        ref[unit, half_rows:2 * half_rows, :] = jnp.where(lo, 0.0, z).astype(BF16)

    off = 0
    for g in range(a_q // LANES):
        z = rope(headnorm(p[:, off:off + LANES], gq_ref[...])) * q_scale
        qa_ref[0, g * t:(g + 1) * t, :] = jnp.where(lo, z, 0.0).astype(BF16)
        qa_ref[1, g * t:(g + 1) * t, :] = jnp.where(lo, 0.0, z).astype(BF16)
        off += LANES
    ka_ref[...] = rope(headnorm(p[:, off:off + a_kv], gk_ref[...])).astype(BF16)
    off += a_kv
    va_ref[...] = p[:, off:off + a_kv].astype(BF16)
    off += a_kv
    for hh in range(b_qk // LANES):
        z = rope(p[:, off:off + LANES]) * q_scale
        split_halves(z, qb_ref, hh, t)
        off += LANES
    for hh in range(b_qk // LANES):
        kb_ref[:, hh * LANES:(hh + 1) * LANES] = rope(p[:, off:off + LANES]).astype(BF16)
        off += LANES
    vb_ref[...] = p[:, off:off + b_v].astype(BF16)


def _qkv_call(xall, mods, g_attn, w_in, gq, gk, cos, sin_a, sin_b, pm, *, layer, n_batch, ctx_tiles,
              a_q, a_kv, b_qk, b_v):
    bsz, s, d = xall.shape
    t = ROW_TILE
    nt = s // t
    d_in = w_in.shape[-1]

    def mod_idx(b, i):
        return (layer, jnp.where(i < ctx_tiles, n_batch, b), 0, 0)

    kern = functools.partial(_qkv_kernel, a_q=a_q, a_kv=a_kv, b_qk=b_qk, b_v=b_v)
    n_ga, n_hb = a_q // LANES, b_qk // LANES
    out_shape = (
        jax.ShapeDtypeStruct((bsz, nt, 2, n_ga * t, LANES), BF16),
        jax.ShapeDtypeStruct((bsz, nt, n_hb, 2 * t, LANES), BF16),
        jax.ShapeDtypeStruct((bsz, s, a_kv), BF16),
        jax.ShapeDtypeStruct((bsz, s, a_kv), BF16),
        jax.ShapeDtypeStruct((bsz, s, b_qk), BF16),
        jax.ShapeDtypeStruct((bsz, s, b_v), BF16),
    )
    return pl.pallas_call(
        kern,
        out_shape=out_shape,
        grid=(bsz, nt),
        in_specs=[
            pl.BlockSpec((None, t, d), lambda b, i: (b, i, 0)),
            pl.BlockSpec((None, None, 6, d), mod_idx),
            pl.BlockSpec((None, 1, d), lambda b, i: (layer, 0, 0)),
            pl.BlockSpec((None, d, d_in), lambda b, i: (layer, 0, 0)),
            pl.BlockSpec((None, 1, LANES), lambda b, i: (layer, 0, 0)),
            pl.BlockSpec((None, 1, LANES), lambda b, i: (layer, 0, 0)),
            pl.BlockSpec((t, LANES), lambda b, i: (i, 0)),
            pl.BlockSpec((t, LANES), lambda b, i: (i, 0)),
            pl.BlockSpec((t, LANES), lambda b, i: (i, 0)),
            pl.BlockSpec((LANES, LANES), lambda b, i: (0, 0)),
        ],
        out_specs=(
            pl.BlockSpec((None, None, 2, n_ga * t, LANES), lambda b, i: (b, i, 0, 0, 0)),
            pl.BlockSpec((None, None, n_hb, 2 * t, LANES), lambda b, i: (b, i, 0, 0, 0)),
            pl.BlockSpec((None, t, a_kv), lambda b, i: (b, i, 0)),
            pl.BlockSpec((None, t, a_kv), lambda b, i: (b, i, 0)),
            pl.BlockSpec((None, t, b_qk), lambda b, i: (b, i, 0)),
            pl.BlockSpec((None, t, b_v), lambda b, i: (b, i, 0)),
        ),
        compiler_params=_cparams(),
        name="qkv_proj",
    )(xall, mods, g_attn, w_in, gq, gk, cos, sin_a, sin_b, pm)


def _attn_kernel(qa_ref, qb_ref, ka_ref, va_ref, kb_ref, vb_ref, x_ref, mod_ref, wo_ref, gsub_ref,
                 lam_ref, o_ref, m_sc, l_sc, acc_sc, mrg_sc, *, tile0, ctx_tiles, ctx_len, lam_init):
    t = x_ref.shape[0]
    s_len = ka_ref.shape[0]
    n_lat_steps = (s_len - ctx_len) // KV_TILE
    is_ctx = (pl.program_id(1) + tile0) < ctx_tiles
    n_steps = jnp.where(is_ctx, 0, n_lat_steps)

    def run_unit(q, k_ref, v_ref, lane0):
        m = q.shape[0]
        m_sc[0:m, :] = jnp.full((m, LANES), -jnp.inf, F32)
        l_sc[0:m, :] = jnp.zeros((m, LANES), F32)
        acc_sc[0:m, :] = jnp.zeros((m, LANES), F32)

        def step(off, tk):
            k = k_ref[pl.ds(off, tk), lane0:lane0 + LANES]
            v = v_ref[pl.ds(off, tk), lane0:lane0 + LANES]
            s = lax.dot_general(q, k, (((1,), (1,)), ((), ())), preferred_element_type=F32)
            m_old = m_sc[0:m, :]
            m_new = jnp.maximum(m_old, jnp.max(s, axis=1, keepdims=True))
            alpha = jnp.exp(m_old - m_new)
            p = jnp.exp(s - jnp.concatenate([m_new] * (tk // LANES), axis=1))
            l_sc[0:m, :] = alpha * l_sc[0:m, :] + jnp.sum(p, axis=1, keepdims=True)
            acc_sc[0:m, :] = alpha * acc_sc[0:m, :] + jnp.dot(
                p.astype(BF16), v, preferred_element_type=F32)
            m_sc[0:m, :] = m_new

        step(0, ctx_len)

        def body(j, carry):
            step(pl.multiple_of(ctx_len + j * KV_TILE, KV_TILE // 2), KV_TILE)
            return carry

        lax.fori_loop(0, n_steps, body, 0)
        return acc_sc[0:m, :] / l_sc[0:m, :]

    lo = lax.broadcasted_iota(jnp.int32, (t, LANES), 1) < HEAD_DIM

    n_ga = qa_ref.shape[1] // t
    o0 = run_unit(qa_ref[0], ka_ref, va_ref, 0)
    for g in range(n_ga):
        mrg_sc[:, g * LANES:(g + 1) * LANES] = o0[g * t:(g + 1) * t, :]
    o1 = run_unit(qa_ref[1], ka_ref, va_ref, 0)
    for g in range(n_ga):
        sl = slice(g * LANES, (g + 1) * LANES)
        mrg_sc[:, sl] = jnp.where(lo, mrg_sc[:, sl], o1[g * t:(g + 1) * t, :])

    lv = lam_ref[...]
    lam = (jnp.exp(jnp.sum(lv[0:1, :] * lv[1:2, :], axis=-1, keepdims=True))
           - jnp.exp(jnp.sum(lv[2:3, :] * lv[3:4, :], axis=-1, keepdims=True)) + lam_init)
    a_w = n_ga * LANES
    for hh in range(qb_ref.shape[0]):
        o = run_unit(qb_ref[hh], kb_ref, vb_ref, hh * LANES)
        diff = o[0:t, :] - lam * o[t:2 * t, :]
        mrg_sc[:, a_w + hh * LANES:a_w + (hh + 1) * LANES] = (
            _rms(diff) * gsub_ref[...] * (1.0 - lam_init))

    y = jnp.dot(mrg_sc[...].astype(BF16), wo_ref[...], preferred_element_type=F32)
    o_ref[...] = x_ref[...] + mod_ref[2:3, :] * y


def _attn_call(qa, qb, ka, va, kb, vb, xall, mods, w_o, g_sub, lamv, *, layer, n_batch, tile0,
               ctx_tiles, ctx_len, lam_init):
    bsz, s, d = xall.shape
    t = ROW_TILE
    nt = s // t - tile0

    def mod_idx(b, i):
        return (layer, jnp.where(i + tile0 < ctx_tiles, n_batch, b), 0, 0)

    kern = functools.partial(_attn_kernel, tile0=tile0, ctx_tiles=ctx_tiles, ctx_len=ctx_len,
                             lam_init=lam_init)
    m_max = max(qa.shape[3], qb.shape[3])
    return pl.pallas_call(
        kern,
        out_shape=jax.ShapeDtypeStruct((bsz, nt * t, d), F32),
        grid=(bsz, nt),
        in_specs=[
            pl.BlockSpec((None, None) + qa.shape[2:], lambda b, i: (b, i + tile0, 0, 0, 0)),
            pl.BlockSpec((None, None) + qb.shape[2:], lambda b, i: (b, i + tile0, 0, 0, 0)),
            pl.BlockSpec((None, s, ka.shape[2]), lambda b, i: (b, 0, 0)),
            pl.BlockSpec((None, s, va.shape[2]), lambda b, i: (b, 0, 0)),
            pl.BlockSpec((None, s, kb.shape[2]), lambda b, i: (b, 0, 0)),
            pl.BlockSpec((None, s, vb.shape[2]), lambda b, i: (b, 0, 0)),
            pl.BlockSpec((None, t, d), lambda b, i: (b, i + tile0, 0)),
            pl.BlockSpec((None, None, 6, d), mod_idx),
            pl.BlockSpec((None, d, d), lambda b, i: (layer, 0, 0)),
            pl.BlockSpec((None, 1, LANES), lambda b, i: (layer, 0, 0)),
            pl.BlockSpec((None, 4, HEAD_DIM), lambda b, i: (layer, 0, 0)),
        ],
        out_specs=pl.BlockSpec((None, t, d), lambda b, i: (b, i, 0)),
        scratch_shapes=[
            pltpu.VMEM((m_max, LANES), F32),
            pltpu.VMEM((m_max, LANES), F32),
            pltpu.VMEM((m_max, LANES), F32),
            pltpu.VMEM((t, d), F32),
        ],
        compiler_params=_cparams(),
        name="attention",
    )(qa, qb, ka, va, kb, vb, xall, mods, w_o, g_sub, lamv)


def _ffn_kernel(x_ref, xp_ref, xn_ref, mod_ref, g_ref, wup_ref, cw_ref, cb_ref, wdn_ref, gfin_ref,
                o_ref, *, tile0, seq_starts, seq_ends, final_norm):
    t = x_ref.shape[0]
    row0 = (pl.program_id(1) + tile0) * t
    prev_ok = functools.reduce(jnp.logical_and, [row0 != r for r in seq_starts])
    next_ok = functools.reduce(jnp.logical_and, [row0 + t != r for r in seq_ends])

    def norm_mod(x):
        return (_rms(x) * g_ref[...]) * (1.0 + mod_ref[4:5, :]) + mod_ref[3:4, :]

    x = x_ref[...]
    h_prev = jnp.where(prev_ok, norm_mod(xp_ref[...]), 0.0)
    h_next = jnp.where(next_ok, norm_mod(xn_ref[...]), 0.0)
    h = jnp.concatenate([h_prev, norm_mod(x), h_next], axis=0).astype(BF16)
    rows = t + 2 * HALO

    y = jnp.zeros((t, x.shape[1]), F32)
    for c in range(wup_ref.shape[0]):
        u = jnp.dot(h, wup_ref[c], preferred_element_type=F32)
        cw = cw_ref[c]
        conv = (pltpu.roll(u, 1, 0) * cw[0:1, :] + u * cw[1:2, :]
                + pltpu.roll(u, rows - 1, 0) * cw[2:3, :] + cb_ref[c])
        conv = conv[HALO:HALO + t, :]
        fc = conv.shape[1] // 2
        gate, val = conv[:, :fc], conv[:, fc:]
        hid = (gate * jax.nn.sigmoid(gate)) * val
        y = y + jnp.dot(hid.astype(BF16), wdn_ref[c], preferred_element_type=F32)

    out = x + mod_ref[5:6, :] * y
    if final_norm:
        out = _rms(out) * gfin_ref[...]
    o_ref[...] = out


def _ffn_call(xall, mods, g_ffn, w_up, conv_w, conv_b, w_down, g_final, *, layer, n_batch, tile0,
              ctx_tiles, ctx_len, final_norm):
    bsz, s, d = xall.shape
    t = ROW_TILE
    nt = s // t - tile0
    hpt = t // HALO
    n_halo = s // HALO
    nch, _, fc2 = w_up.shape[1:]

    def mod_idx(b, i):
        return (layer, jnp.where(i + tile0 < ctx_tiles, n_batch, b), 0, 0)

    kern = functools.partial(_ffn_kernel, tile0=tile0, seq_starts=(0, ctx_len),
                             seq_ends=(ctx_len, s), final_norm=final_norm)
    const1 = pl.Buffered(1)
    return pl.pallas_call(
        kern,
        out_shape=jax.ShapeDtypeStruct((bsz, nt * t, d), F32),
        grid=(bsz, nt),
        in_specs=[
            pl.BlockSpec((None, t, d), lambda b, i: (b, i + tile0, 0)),
            pl.BlockSpec((None, HALO, d),
                         lambda b, i: (b, jnp.maximum((i + tile0) * hpt - 1, 0), 0)),
            pl.BlockSpec((None, HALO, d),
                         lambda b, i: (b, jnp.minimum((i + tile0 + 1) * hpt, n_halo - 1), 0)),
            pl.BlockSpec((None, None, 6, d), mod_idx),
            pl.BlockSpec((None, 1, d), lambda b, i: (layer, 0, 0)),
            pl.BlockSpec((None, nch, d, fc2), lambda b, i: (layer, 0, 0, 0), pipeline_mode=const1),
            pl.BlockSpec((None, nch, CONV_W, fc2), lambda b, i: (layer, 0, 0, 0)),
            pl.BlockSpec((None, nch, 1, fc2), lambda b, i: (layer, 0, 0, 0)),
            pl.BlockSpec((None, nch, fc2 // 2, d), lambda b, i: (layer, 0, 0, 0),
                         pipeline_mode=const1),
            pl.BlockSpec((1, d), lambda b, i: (0, 0)),
        ],
        out_specs=pl.BlockSpec((None, t, d), lambda b, i: (b, i, 0)),
        compiler_params=_cparams(),
        name="conv_ffn",
    )(xall, xall, xall, mods, g_ffn, w_up, conv_w, conv_b, w_down, g_final)


def _rope_tables(ctx_len, n_lat):
    rows = n_lat // GRID_W
    row = jnp.repeat(jnp.arange(rows, dtype=F32), GRID_W)
    col = jnp.tile(jnp.arange(GRID_W, dtype=F32), rows)
    quarter = HEAD_DIM // 4
    inv_freq = ROPE_THETA ** (-jnp.arange(quarter, dtype=F32) / quarter)
    ang_r = row[:, None] * inv_freq[None, :]
    ang_c = col[:, None] * inv_freq[None, :]
    cos = jnp.concatenate([jnp.cos(ang_r), jnp.cos(ang_r), jnp.cos(ang_c), jnp.cos(ang_c)], axis=-1)
    sin = jnp.concatenate([jnp.sin(ang_r), jnp.sin(ang_r), jnp.sin(ang_c), jnp.sin(ang_c)], axis=-1)
    cos = jnp.concatenate([jnp.ones((ctx_len, HEAD_DIM), F32), cos], axis=0)
    sin = jnp.concatenate([jnp.zeros((ctx_len, HEAD_DIM), F32), sin], axis=0)
    cos = jnp.tile(cos, (1, LANES // HEAD_DIM))
    sin = jnp.tile(sin, (1, LANES // HEAD_DIM))
    upper = (jnp.arange(LANES) // quarter) % 2 == 1
    sin_a = jnp.where(upper[None, :], sin, 0.0)
    sin_b = jnp.where(upper[None, :], 0.0, -sin)
    return cos, sin_a, sin_b


def kernel(x, c, ctx, c_ctx, w_ada, b_ada, g_attn, w_in, g_qa, g_ka, lam_q1, lam_k1, lam_q2, lam_k2,
           g_sub, w_o, g_ffn, w_up, conv_w, conv_b, w_down, g_final):
    bsz, n_lat, d = x.shape
    ctx_len = ctx.shape[1]
    depth = w_ada.shape[0]
    t = ROW_TILE
    a_heads, a_kv_heads, b_heads = d // 128, d // 512, d // 256
    a_q, a_kv = a_heads * HEAD_DIM, a_kv_heads * HEAD_DIM
    b_qk, b_v = b_heads * 2 * HEAD_DIM, b_heads * 2 * HEAD_DIM
    a_group = a_heads // a_kv_heads
    f = w_down.shape[1]
    assert a_kv == LANES and n_lat % KV_TILE == 0 and ctx_len % t == 0 and n_lat % t == 0
    assert f % (FFN_CHUNKS * LANES) == 0
    ctx_tiles = ctx_len // t

    xall = jnp.concatenate([ctx, x], axis=1)

    pad = (-(bsz + 1)) % 8
    c_all = jnp.concatenate([c, c_ctx[None, :], jnp.zeros((pad, d), F32)], axis=0)
    mods = _ada_call(c_all, w_ada, b_ada).reshape(depth, bsz + 1 + pad, 6, d)

    head_order = [h for g in range(a_group) for h in (g, g + a_group)]
    a_perm = np.concatenate([np.arange(h * HEAD_DIM, (h + 1) * HEAD_DIM) for h in head_order])
    in_perm = np.concatenate([a_perm, np.arange(a_q, w_in.shape[-1])])
    w_in_p = w_in[:, :, in_perm].astype(BF16)
    out_perm = np.concatenate([a_perm, np.arange(a_q, w_o.shape[1])])
    w_o_p = w_o[:, out_perm, :].astype(BF16)

    fc = f // FFN_CHUNKS

    def chunk_cols(a):
        g_part = a[..., :f].reshape(a.shape[:-1] + (FFN_CHUNKS, fc))
        v_part = a[..., f:].reshape(a.shape[:-1] + (FFN_CHUNKS, fc))
        return jnp.concatenate([g_part, v_part], axis=-1)

    w_up_c = jnp.moveaxis(chunk_cols(w_up.astype(BF16)), 2, 1)
    conv_w_c = jnp.moveaxis(chunk_cols(conv_w), 2, 1)
    conv_b_c = chunk_cols(conv_b)[:, :, None, :]
    w_down_c = w_down.astype(BF16).reshape(depth, FFN_CHUNKS, fc, d)

    rep = LANES // HEAD_DIM
    gq = jnp.tile(g_qa, (1, rep))[:, None, :]
    gk = jnp.tile(g_ka, (1, rep))[:, None, :]
    lamv = jnp.stack([lam_q1, lam_k1, lam_q2, lam_k2], axis=1)
    seg = np.arange(LANES) // HEAD_DIM
    pm = jnp.asarray((seg[:, None] == seg[None, :]) / HEAD_DIM, BF16)
    cos, sin_a, sin_b = _rope_tables(ctx_len, n_lat)

    for l in range(depth):
        last = l == depth - 1
        tile0 = ctx_tiles if last else 0
        lam_init = 0.8 - 0.6 * math.exp(-0.3 * l)
        qa, qb, ka, va, kb, vb = _qkv_call(
            xall, mods, g_attn[:, None, :], w_in_p, gq, gk, cos, sin_a, sin_b, pm, layer=l,
            n_batch=bsz, ctx_tiles=ctx_tiles, a_q=a_q, a_kv=a_kv, b_qk=b_qk, b_v=b_v)
        x_mix = _attn_call(qa, qb, ka, va, kb, vb, xall, mods, w_o_p, g_sub[:, None, :], lamv,
                           layer=l, n_batch=bsz, tile0=tile0, ctx_tiles=ctx_tiles, ctx_len=ctx_len,
                           lam_init=lam_init)
        xall = _ffn_call(x_mix, mods, g_ffn[:, None, :], w_up_c, conv_w_c, conv_b_c, w_down_c,
                         g_final[None, :], layer=l, n_batch=bsz, tile0=0,
                         ctx_tiles=0 if last else ctx_tiles, ctx_len=0 if last else ctx_len,
                         final_norm=last)
    return xall
```

```python
import functools
import math

import jax
import jax.numpy as jnp
import numpy as np
from jax import lax
from jax.experimental import pallas as pl
from jax.experimental.pallas import tpu as pltpu

F32 = jnp.float32
BF16 = jnp.bfloat16

HEAD_DIM = 64
LANES = 128
GRID_W = 64
ROPE_THETA = 10000.0
EPS = 1e-6
CONV_W = 3
QUARTER = HEAD_DIM // 4

ROW_TILE = 256
Q_BLOCKS_PER_UNIT = 2
KEY_CHUNK = 1024
FFN_CHUNKS = 2
HALO = 8
VMEM_LIMIT = 56 * 1024 * 1024


def _cparams(**flags):
    return pltpu.CompilerParams(
        dimension_semantics=("parallel", "arbitrary"),
        vmem_limit_bytes=VMEM_LIMIT,
        flags=flags or None,
    )


def _rms(x):
    return x * lax.rsqrt(jnp.mean(x * x, axis=-1, keepdims=True) + EPS)


def _lane_tile(a, width):
    return jnp.concatenate([a] * (width // LANES), axis=1)


def _ada_kernel(c_ref, w_ref, b_ref, o_ref):
    c = c_ref[...]
    s = c * jax.nn.sigmoid(c)
    o_ref[...] = jnp.dot(s.astype(BF16), w_ref[...].astype(BF16),
                         preferred_element_type=F32) + b_ref[...]


def _ada_call(c_all, w_ada, b_ada):
    depth, d, d6 = w_ada.shape
    rows = c_all.shape[0]
    tn = d6 // 4
    return pl.pallas_call(
        _ada_kernel,
        out_shape=jax.ShapeDtypeStruct((depth, rows, d6), F32),
        grid=(depth, d6 // tn),
        in_specs=[
            pl.BlockSpec((rows, d), lambda l, j: (0, 0)),
            pl.BlockSpec((None, d, tn), lambda l, j: (l, 0, j)),
            pl.BlockSpec((None, 1, tn), lambda l, j: (l, 0, j)),
        ],
        out_specs=pl.BlockSpec((None, rows, tn), lambda l, j: (l, 0, j)),
        compiler_params=_cparams(),
        name="adaln",
    )(c_all, w_ada, b_ada.reshape(depth, 1, d6))


def _qkv_kernel(*refs, a_heads, a_kv_heads, b_heads, use_rope):
    if use_rope:
        (x_ref, mod_ref, g_ref, wk_ref, wqv_ref, gq_ref, gk_ref, pm_ref,
         cos_ref, sa_ref, sb_ref, cost_ref, sint_ref, qt_ref, k_ref, vt_ref) = refs
    else:
        (x_ref, mod_ref, g_ref, wk_ref, wqv_ref, gq_ref, gk_ref, pm_ref,
         qt_ref, k_ref, vt_ref) = refs
    t = x_ref.shape[0]
    x = x_ref[...]
    h = (_rms(x) * g_ref[...]) * (1.0 + mod_ref[1:2, :]) + mod_ref[0:1, :]
    pk = jnp.dot(h.astype(BF16), wk_ref[...], preferred_element_type=F32)
    pt = jnp.dot(wqv_ref[...], h.T.astype(BF16), preferred_element_type=F32)

    def rope_k(z):
        if not use_rope:
            return z
        return (z * cos_ref[...] + pltpu.roll(z, QUARTER, 1) * sa_ref[...]
                + pltpu.roll(z, LANES - QUARTER, 1) * sb_ref[...])

    ka = pk[:, 0:LANES]
    sq = ka * ka
    hi = sq.astype(BF16)
    lo = (sq - hi.astype(F32)).astype(BF16)
    ms = (jnp.dot(hi, pm_ref[...], preferred_element_type=F32)
          + jnp.dot(lo, pm_ref[...], preferred_element_type=F32))
    k_ref[0] = rope_k(ka * lax.rsqrt(ms + EPS) * gk_ref[...]).astype(BF16)
    for hh in range(b_heads):
        k_ref[1 + hh] = rope_k(pk[:, (1 + hh) * LANES:(2 + hh) * LANES]).astype(BF16)

    def rope_q(z):
        if not use_rope:
            return z
        swap = jnp.concatenate([z[QUARTER:2 * QUARTER], z[0:QUARTER],
                                z[3 * QUARTER:4 * QUARTER], z[2 * QUARTER:3 * QUARTER]], axis=0)
        return z * cost_ref[...] + swap * sint_ref[...]

    q_scale = HEAD_DIM ** -0.5 * math.log2(math.e)
    zero = jnp.zeros((HEAD_DIM, t), BF16)
    gq = _lane_tile(gq_ref[...], t)

    def put_q(blk, half, z):
        qt_ref[blk, half * HEAD_DIM:(half + 1) * HEAD_DIM, :] = (z * q_scale).astype(BF16)
        qt_ref[blk, (1 - half) * HEAD_DIM:(2 - half) * HEAD_DIM, :] = zero

    group = a_heads // a_kv_heads
    for hh in range(a_heads):
        z = pt[hh * HEAD_DIM:(hh + 1) * HEAD_DIM, :]
        z = z * lax.rsqrt(jnp.mean(z * z, axis=0, keepdims=True) + EPS) * gq
        put_q(hh, hh // group, rope_q(z))
    off = a_heads * HEAD_DIM
    vt_ref[0] = pt[off:off + LANES, :].astype(BF16)
    off += LANES
    for hh in range(b_heads):
        for j in range(2):
            put_q(a_heads + 2 * hh + j, j, rope_q(pt[off:off + HEAD_DIM, :]))
            off += HEAD_DIM
    for hh in range(b_heads):
        vt_ref[1 + hh] = pt[off:off + LANES, :].astype(BF16)
        off += LANES


def _qkv_call(x, mods, g_attn, w_k, w_qvt, gq, gk, pm, rope, *, layer, mod_row, a_heads,
              a_kv_heads, b_heads):
    bsz, s, d = x.shape
    t = ROW_TILE
    nt = s // t
    n_kv = 1 + b_heads
    n_qblk = a_heads + 2 * b_heads
    use_rope = rope is not None
    kern = functools.partial(_qkv_kernel, a_heads=a_heads, a_kv_heads=a_kv_heads, b_heads=b_heads,
                             use_rope=use_rope)
    in_specs = [
        pl.BlockSpec((None, t, d), lambda b, i: (b, i, 0)),
        pl.BlockSpec((None, None, 6, d), lambda b, i: (layer, mod_row(b), 0, 0)),
        pl.BlockSpec((None, 1, d), lambda b, i: (layer, 0, 0)),
        pl.BlockSpec((None,) + w_k.shape[1:], lambda b, i: (layer, 0, 0)),
        pl.BlockSpec((None,) + w_qvt.shape[1:], lambda b, i: (layer, 0, 0)),
        pl.BlockSpec((None, HEAD_DIM, LANES), lambda b, i: (layer, 0, 0)),
        pl.BlockSpec((None, 1, LANES), lambda b, i: (layer, 0, 0)),
        pl.BlockSpec((LANES, LANES), lambda b, i: (0, 0)),
    ]
    args = [x, mods, g_attn, w_k, w_qvt, gq, gk, pm]
    if use_rope:
        in_specs += [pl.BlockSpec((t, LANES), lambda b, i: (i, 0))] * 3
        in_specs += [pl.BlockSpec((HEAD_DIM, t), lambda b, i: (0, i))] * 2
        args += list(rope)
    return pl.pallas_call(
        kern,
        out_shape=(
            jax.ShapeDtypeStruct((bsz, nt, n_qblk, LANES, t), BF16),
            jax.ShapeDtypeStruct((bsz, n_kv, s, LANES), BF16),
            jax.ShapeDtypeStruct((bsz, n_kv, LANES, s), BF16),
        ),
        grid=(bsz, nt),
        in_specs=in_specs,
        out_specs=(
            pl.BlockSpec((None, None, n_qblk, LANES, t), lambda b, i: (b, i, 0, 0, 0)),
            pl.BlockSpec((None, n_kv, t, LANES), lambda b, i: (b, 0, i, 0)),
            pl.BlockSpec((None, n_kv, LANES, t), lambda b, i: (b, 0, 0, i)),
        ),
        compiler_params=_cparams(),
        name="qkv_proj_x" if use_rope else "qkv_proj_c",
    )(*args)


def _attn_kernel(*refs, n_seg, a_heads, a_kv_heads, b_heads, lam_init):
    qt_ref = refs[0]
    k_refs = refs[1:1 + n_seg]
    vt_refs = refs[1 + n_seg:1 + 2 * n_seg]
    x_ref, mod_ref, wot_ref, gsub_ref, lam_ref, o_ref, ot_sc, mrg_sc = refs[1 + 2 * n_seg:]
    t = x_ref.shape[0]
    qpu = Q_BLOCKS_PER_UNIT
    n_a_units = a_heads // qpu
    n_units = n_a_units + b_heads

    def unit(u, carry):
        kv = jnp.maximum(u - (n_a_units - 1), 0)
        qt = jnp.concatenate([qt_ref[qpu * u + j] for j in range(qpu)], axis=1)
        chunks = []
        for k_ref, vt_ref in zip(k_refs, vt_refs):
            n_keys = k_ref.shape[1]
            ck = min(n_keys, KEY_CHUNK)
            chunks += [(k_ref, vt_ref, c0, ck) for c0 in range(0, n_keys, ck)]

        def scores(chunk):
            k_ref, _, c0, ck = chunk
            return jnp.dot(k_ref[kv, c0:c0 + ck, :], qt, preferred_element_type=F32)

        m = den = acc = None
        s = scores(chunks[0])
        for ci, (_, vt_ref, c0, ck) in enumerate(chunks):
            s_next = scores(chunks[ci + 1]) if ci + 1 < len(chunks) else None
            m_c = jnp.max(s, axis=0, keepdims=True)
            m_new = m_c if m is None else jnp.maximum(m, m_c)
            p = jnp.exp2(s - m_new)
            den_c = jnp.sum(p, axis=0, keepdims=True)
            acc_c = jnp.dot(vt_ref[kv, :, c0:c0 + ck], p.astype(BF16),
                            preferred_element_type=F32)
            if m is None:
                den, acc = den_c, acc_c
            else:
                alpha = jnp.exp2(m - m_new)
                den, acc = alpha * den + den_c, alpha * acc + acc_c
            m, s = m_new, s_next
        ot_sc[u] = acc * (1.0 / den)
        return carry

    lax.fori_loop(0, n_units, unit, 0)

    group = a_heads // a_kv_heads
    for hh in range(a_heads):
        kvh = hh // group
        blk = ot_sc[hh // qpu, kvh * HEAD_DIM:(kvh + 1) * HEAD_DIM, (hh % qpu) * t:(hh % qpu + 1) * t]
        mrg_sc[hh * HEAD_DIM:(hh + 1) * HEAD_DIM, :] = blk.astype(BF16)

    lv = lam_ref[...]
    lam = (jnp.exp(jnp.sum(lv[0:1, :] * lv[1:2, :], axis=-1, keepdims=True))
           - jnp.exp(jnp.sum(lv[2:3, :] * lv[3:4, :], axis=-1, keepdims=True)) + lam_init)
    gsub = _lane_tile(gsub_ref[...], t)
    a_w = a_heads * HEAD_DIM
    for hh in range(b_heads):
        o = ot_sc[n_a_units + hh]
        diff = o[:, 0:t] - lam * o[:, t:2 * t]
        sub = diff * lax.rsqrt(jnp.mean(diff * diff, axis=0, keepdims=True) + EPS) * gsub
        mrg_sc[a_w + hh * LANES:a_w + (hh + 1) * LANES, :] = (sub * (1.0 - lam_init)).astype(BF16)

    yt = jnp.dot(wot_ref[...], mrg_sc[...], preferred_element_type=F32)
    o_ref[...] = x_ref[...] + mod_ref[2:3, :] * yt.T


def _attn_call(qt, ks, vts, x, mods, w_ot, gsub, lamv, *, layer, mod_row, a_heads, a_kv_heads,
               b_heads, lam_init):
    bsz, s, d = x.shape
    t = ROW_TILE
    nt = s // t
    n_seg = len(ks)
    assert Q_BLOCKS_PER_UNIT == 2 and a_heads % (2 * a_kv_heads) == 0
    n_units = a_heads // Q_BLOCKS_PER_UNIT + b_heads
    kern = functools.partial(_attn_kernel, n_seg=n_seg, a_heads=a_heads, a_kv_heads=a_kv_heads,
                             b_heads=b_heads, lam_init=lam_init)
    once = pl.Buffered(1)
    in_specs = [pl.BlockSpec((None, None) + qt.shape[2:], lambda b, i: (b, i, 0, 0, 0))]
    in_specs += [pl.BlockSpec((None,) + k.shape[1:], lambda b, i: (b, 0, 0, 0), pipeline_mode=once)
                 for k in ks]
    in_specs += [pl.BlockSpec((None,) + v.shape[1:], lambda b, i: (b, 0, 0, 0), pipeline_mode=once)
                 for v in vts]
    in_specs += [
        pl.BlockSpec((None, t, d), lambda b, i: (b, i, 0)),
        pl.BlockSpec((None, None, 6, d), lambda b, i: (layer, mod_row(b), 0, 0)),
        pl.BlockSpec((None, d, d), lambda b, i: (layer, 0, 0), pipeline_mode=once),
        pl.BlockSpec((None, LANES, LANES), lambda b, i: (layer, 0, 0)),
        pl.BlockSpec((None, 4, HEAD_DIM), lambda b, i: (layer, 0, 0)),
    ]
    return pl.pallas_call(
        kern,
        out_shape=jax.ShapeDtypeStruct((bsz, s, d), F32),
        grid=(bsz, nt),
        in_specs=in_specs,
        out_specs=pl.BlockSpec((None, t, d), lambda b, i: (b, i, 0)),
        scratch_shapes=[
            pltpu.VMEM((n_units, LANES, Q_BLOCKS_PER_UNIT * t), F32),
            pltpu.VMEM((d, t), BF16),
        ],
        compiler_params=_cparams(),
        name="attention_x" if n_seg == 2 else "attention_c",
    )(qt, *ks, *vts, x, mods, w_ot, gsub, lamv)


def _ffn_kernel(x_ref, xp_ref, xn_ref, mod_ref, g_ref, wup_ref, cw_ref, cb_ref, wdn_ref, gfin_ref,
                o_ref, *, final_norm):
    t = x_ref.shape[0]
    i = pl.program_id(1)

    def norm_mod(x):
        return (_rms(x) * g_ref[...]) * (1.0 + mod_ref[4:5, :]) + mod_ref[3:4, :]

    x = x_ref[...]
    h_prev = jnp.where(i > 0, norm_mod(xp_ref[...]), 0.0)
    h_next = jnp.where(i < pl.num_programs(1) - 1, norm_mod(xn_ref[...]), 0.0)
    h = jnp.concatenate([h_prev, norm_mod(x), h_next], axis=0).astype(BF16)
    rows = t + 2 * HALO

    y = jnp.zeros((t, x.shape[1]), F32)
    for c in range(wup_ref.shape[0]):
        u = jnp.dot(h, wup_ref[c], preferred_element_type=F32)
        cw = cw_ref[c]
        conv = (pltpu.roll(u, 1, 0) * cw[0:1, :] + u * cw[1:2, :]
                + pltpu.roll(u, rows - 1, 0) * cw[2:3, :] + cb_ref[c])
        conv = conv[HALO:HALO + t, :]
        fc = conv.shape[1] // 2
        gate, val = conv[:, :fc], conv[:, fc:]
        hid = (gate * jax.nn.sigmoid(gate)) * val
        y = y + jnp.dot(hid.astype(BF16), wdn_ref[c], preferred_element_type=F32)

    out = x + mod_ref[5:6, :] * y
    if final_norm:
        out = _rms(out) * gfin_ref[...]
    o_ref[...] = out


def _ffn_call(x, mods, g_ffn, w_up, conv_w, conv_b, w_down, g_final, *, layer, mod_row, final_norm):
    bsz, s, d = x.shape
    t = ROW_TILE
    nt = s // t
    hpt = t // HALO
    n_halo = s // HALO
    nch, _, fc2 = w_up.shape[1:]
    kern = functools.partial(_ffn_kernel, final_norm=final_norm)
    once = pl.Buffered(1)
    return pl.pallas_call(
        kern,
        out_shape=jax.ShapeDtypeStruct((bsz, s, d), F32),
        grid=(bsz, nt),
        in_specs=[
            pl.BlockSpec((None, t, d), lambda b, i: (b, i, 0)),
            pl.BlockSpec((None, HALO, d), lambda b, i: (b, jnp.maximum(i * hpt - 1, 0), 0)),
            pl.BlockSpec((None, HALO, d),
                         lambda b, i: (b, jnp.minimum((i + 1) * hpt, n_halo - 1), 0)),
            pl.BlockSpec((None, None, 6, d), lambda b, i: (layer, mod_row(b), 0, 0)),
            pl.BlockSpec((None, 1, d), lambda b, i: (layer, 0, 0)),
            pl.BlockSpec((None, nch, d, fc2), lambda b, i: (layer, 0, 0, 0), pipeline_mode=once),
            pl.BlockSpec((None, nch, CONV_W, fc2), lambda b, i: (layer, 0, 0, 0)),
            pl.BlockSpec((None, nch, 1, fc2), lambda b, i: (layer, 0, 0, 0)),
            pl.BlockSpec((None, nch, fc2 // 2, d), lambda b, i: (layer, 0, 0, 0),
                         pipeline_mode=once),
            pl.BlockSpec((1, d), lambda b, i: (0, 0)),
        ],
        out_specs=pl.BlockSpec((None, t, d), lambda b, i: (b, i, 0)),
        compiler_params=_cparams(),
        name="conv_ffn",
    )(x, x, x, mods, g_ffn, w_up, conv_w, conv_b, w_down, g_final)


def _rope_tables(n_lat):
    rows = n_lat // GRID_W
    row = jnp.repeat(jnp.arange(rows, dtype=F32), GRID_W)
    col = jnp.tile(jnp.arange(GRID_W, dtype=F32), rows)
    inv_freq = ROPE_THETA ** (-jnp.arange(QUARTER, dtype=F32) / QUARTER)
    ang_r = row[:, None] * inv_freq[None, :]
    ang_c = col[:, None] * inv_freq[None, :]
    cos = jnp.concatenate([jnp.cos(ang_r), jnp.cos(ang_r), jnp.cos(ang_c), jnp.cos(ang_c)], axis=-1)
    sin = jnp.concatenate([jnp.sin(ang_r), jnp.sin(ang_r), jnp.sin(ang_c), jnp.sin(ang_c)], axis=-1)
    upper = (np.arange(HEAD_DIM) // QUARTER) % 2 == 1
    cos_t = cos.T
    sin_t = jnp.where(upper[None, :], sin, -sin).T
    rep = LANES // HEAD_DIM
    cos2, sin2 = jnp.tile(cos, (1, rep)), jnp.tile(sin, (1, rep))
    upper2 = np.tile(upper, rep)[None, :]
    sin_a = jnp.where(upper2, sin2, 0.0)
    sin_b = jnp.where(upper2, 0.0, -sin2)
    return cos2, sin_a, sin_b, cos_t, sin_t


def kernel(x, c, ctx, c_ctx, w_ada, b_ada, g_attn, w_in, g_qa, g_ka, lam_q1, lam_k1, lam_q2, lam_k2,
           g_sub, w_o, g_ffn, w_up, conv_w, conv_b, w_down, g_final):
    bsz, n_lat, d = x.shape
    ctx_len = ctx.shape[1]
    depth = w_ada.shape[0]
    t = ROW_TILE
    a_heads, a_kv_heads, b_heads = d // 128, d // 512, d // 256
    a_q, a_kv = a_heads * HEAD_DIM, a_kv_heads * HEAD_DIM
    b_qk, b_v = b_heads * 2 * HEAD_DIM, b_heads * 2 * HEAD_DIM
    f = w_down.shape[1]
    assert a_kv == LANES and ctx_len % t == 0 and n_lat % t == 0
    assert f % (FFN_CHUNKS * LANES) == 0

    pad = (-(bsz + 1)) % 8
    c_all = jnp.concatenate([c, c_ctx[None, :], jnp.zeros((pad, d), F32)], axis=0)
    mods = _ada_call(c_all, w_ada, b_ada).reshape(depth, bsz + 1 + pad, 6, d)
    row_x = lambda b: b
    row_c = lambda b: bsz

    o_ak, o_av, o_bq, o_bk, o_bv = np.cumsum([a_q, a_kv, a_kv, b_qk, b_qk])
    w_k = jnp.concatenate([w_in[:, :, o_ak:o_av], w_in[:, :, o_bk:o_bv]], axis=-1).astype(BF16)
    w_qvt = jnp.swapaxes(jnp.concatenate(
        [w_in[:, :, :o_ak], w_in[:, :, o_av:o_bq], w_in[:, :, o_bq:o_bk], w_in[:, :, o_bv:]],
        axis=-1), 1, 2).astype(BF16)
    w_ot = jnp.swapaxes(w_o, 1, 2).astype(BF16)

    fc = f // FFN_CHUNKS

    def chunk_cols(a):
        g_part = a[..., :f].reshape(a.shape[:-1] + (FFN_CHUNKS, fc))
        v_part = a[..., f:].reshape(a.shape[:-1] + (FFN_CHUNKS, fc))
        return jnp.concatenate([g_part, v_part], axis=-1)

    w_up_c = jnp.moveaxis(chunk_cols(w_up.astype(BF16)), 2, 1)
    conv_w_c = jnp.moveaxis(chunk_cols(conv_w), 2, 1)
    conv_b_c = chunk_cols(conv_b)[:, :, None, :]
    w_down_c = w_down.astype(BF16).reshape(depth, FFN_CHUNKS, fc, d)

    gq = jnp.broadcast_to(g_qa[:, :, None], (depth, HEAD_DIM, LANES))
    gk = jnp.tile(g_ka, (1, LANES // HEAD_DIM))[:, None, :]
    gsub = jnp.broadcast_to(g_sub[:, :, None], (depth, LANES, LANES))
    lamv = jnp.stack([lam_q1, lam_k1, lam_q2, lam_k2], axis=1)
    seg = np.arange(LANES) // HEAD_DIM
    pm = jnp.asarray((seg[:, None] == seg[None, :]) / HEAD_DIM, BF16)
    rope = _rope_tables(n_lat)
    heads = dict(a_heads=a_heads, a_kv_heads=a_kv_heads, b_heads=b_heads)

    xl, xc = x, ctx
    for l in range(depth):
        last = l == depth - 1
        lam_init = 0.8 - 0.6 * math.exp(-0.3 * l)
        qc, kc, vc = _qkv_call(xc, mods, g_attn[:, None, :], w_k, w_qvt, gq, gk, pm, None,
                               layer=l, mod_row=row_c, **heads)
        qx, kx, vx = _qkv_call(xl, mods, g_attn[:, None, :], w_k, w_qvt, gq, gk, pm, rope,
                               layer=l, mod_row=row_x, **heads)
        xl = _attn_call(qx, [kc, kx], [vc, vx], xl, mods, w_ot, gsub, lamv, layer=l, mod_row=row_x,
                        lam_init=lam_init, **heads)
        xl = _ffn_call(xl, mods, g_ffn[:, None, :], w_up_c, conv_w_c, conv_b_c, w_down_c,
                       g_final[None, :], layer=l, mod_row=row_x, final_norm=last)
        if not last:
            xc = _attn_call(qc, [kc], [vc], xc, mods, w_ot, gsub, lamv, layer=l, mod_row=row_c,
                            lam_init=lam_init, **heads)
            xc = _ffn_call(xc, mods, g_ffn[:, None, :], w_up_c, conv_w_c, conv_b_c, w_down_c,
                           g_final[None, :], layer=l, mod_row=row_c, final_norm=False)
    return xl
```

```python
import functools
import math

import jax
import jax.numpy as jnp
import numpy as np
from jax import lax
from jax.experimental import pallas as pl
from jax.experimental.pallas import tpu as pltpu

F32 = jnp.float32
BF16 = jnp.bfloat16

HEAD_DIM = 64
LANES = 128
GRID_W = 64
ROPE_THETA = 10000.0
EPS = 1e-6
CONV_W = 3
QUARTER = HEAD_DIM // 4

ROW_TILE = 256
Q_BLOCKS_PER_UNIT = 2
KEY_CHUNK = 512
SCORE_LOOKAHEAD = 2
UNITS_PER_STEP = 2
V_ROWS = LANES + 16
FFN_CHUNKS = 2
HALO = 8
VMEM_LIMIT = 56 * 1024 * 1024


def _cparams():
    return pltpu.CompilerParams(
        dimension_semantics=("parallel", "arbitrary"),
        vmem_limit_bytes=VMEM_LIMIT,
    )


def _rms(x):
    return x * lax.rsqrt(jnp.mean(x * x, axis=-1, keepdims=True) + EPS)


def _lane_tile(a, width):
    return jnp.concatenate([a] * (width // LANES), axis=1)


def _ada_kernel(c_ref, w_ref, b_ref, o_ref):
    c = c_ref[...]
    s = c * jax.nn.sigmoid(c)
    o_ref[...] = jnp.dot(s.astype(BF16), w_ref[...].astype(BF16),
                         preferred_element_type=F32) + b_ref[...]


def _ada_call(c_all, w_ada, b_ada):
    depth, d, d6 = w_ada.shape
    rows = c_all.shape[0]
    tn = d6 // 4
    return pl.pallas_call(
        _ada_kernel,
        out_shape=jax.ShapeDtypeStruct((depth, rows, d6), F32),
        grid=(depth, d6 // tn),
        in_specs=[
            pl.BlockSpec((rows, d), lambda l, j: (0, 0)),
            pl.BlockSpec((None, d, tn), lambda l, j: (l, 0, j)),
            pl.BlockSpec((None, 1, tn), lambda l, j: (l, 0, j)),
        ],
        out_specs=pl.BlockSpec((None, rows, tn), lambda l, j: (l, 0, j)),
        compiler_params=_cparams(),
        name="adaln",
    )(c_all, w_ada, b_ada.reshape(depth, 1, d6))


def _qkv_kernel(*refs, a_heads, a_kv_heads, b_heads, use_rope):
    if use_rope:
        (x_ref, mod_ref, g_ref, wk_ref, wqv_ref, gq_ref, gk_ref, pm_ref,
         cos_ref, sa_ref, sb_ref, cost_ref, sint_ref, qt_ref, k_ref, vt_ref) = refs
    else:
        (x_ref, mod_ref, g_ref, wk_ref, wqv_ref, gq_ref, gk_ref, pm_ref,
         qt_ref, k_ref, vt_ref) = refs
    t = x_ref.shape[0]
    x = x_ref[...]
    h = (_rms(x) * g_ref[...]) * (1.0 + mod_ref[1:2, :]) + mod_ref[0:1, :]
    pk = jnp.dot(h.astype(BF16), wk_ref[...], preferred_element_type=F32)
    pt = jnp.dot(wqv_ref[...], h.T.astype(BF16), preferred_element_type=F32)

    def rope_k(z):
        if not use_rope:
            return z
        return (z * cos_ref[...] + pltpu.roll(z, QUARTER, 1) * sa_ref[...]
                + pltpu.roll(z, LANES - QUARTER, 1) * sb_ref[...])

    ka = pk[:, 0:LANES]
    sq = ka * ka
    hi = sq.astype(BF16)
    lo = (sq - hi.astype(F32)).astype(BF16)
    ms = (jnp.dot(hi, pm_ref[...], preferred_element_type=F32)
          + jnp.dot(lo, pm_ref[...], preferred_element_type=F32))
    k_ref[0] = rope_k(ka * lax.rsqrt(ms + EPS) * gk_ref[...]).astype(BF16)
    for hh in range(b_heads):
        k_ref[1 + hh] = rope_k(pk[:, (1 + hh) * LANES:(2 + hh) * LANES]).astype(BF16)

    def rope_q(z):
        if not use_rope:
            return z
        swap = jnp.concatenate([z[QUARTER:2 * QUARTER], z[0:QUARTER],
                                z[3 * QUARTER:4 * QUARTER], z[2 * QUARTER:3 * QUARTER]], axis=0)
        return z * cost_ref[...] + swap * sint_ref[...]

    q_scale = HEAD_DIM ** -0.5 * math.log2(math.e)
    zero = jnp.zeros((HEAD_DIM, t), BF16)
    gq = _lane_tile(gq_ref[...], t)

    def put_q(blk, half, z):
        qt_ref[blk, half * HEAD_DIM:(half + 1) * HEAD_DIM, :] = (z * q_scale).astype(BF16)
        qt_ref[blk, (1 - half) * HEAD_DIM:(2 - half) * HEAD_DIM, :] = zero

    group = a_heads // a_kv_heads
    for hh in range(a_heads):
        z = pt[hh * HEAD_DIM:(hh + 1) * HEAD_DIM, :]
        z = z * lax.rsqrt(jnp.mean(z * z, axis=0, keepdims=True) + EPS) * gq
        put_q(hh, hh // group, rope_q(z))
    off = a_heads * HEAD_DIM
    ones = jnp.ones((V_ROWS - LANES, t), BF16)
    vt_ref[0, 0:LANES, :] = pt[off:off + LANES, :].astype(BF16)
    vt_ref[0, LANES:V_ROWS, :] = ones
    off += LANES
    for hh in range(b_heads):
        for j in range(2):
            put_q(a_heads + 2 * hh + j, j, rope_q(pt[off:off + HEAD_DIM, :]))
            off += HEAD_DIM
    for hh in range(b_heads):
        vt_ref[1 + hh, 0:LANES, :] = pt[off:off + LANES, :].astype(BF16)
        vt_ref[1 + hh, LANES:V_ROWS, :] = ones
        off += LANES


def _qkv_call(x, mods, g_attn, w_k, w_qvt, gq, gk, pm, rope, *, layer, mod_row, a_heads,
              a_kv_heads, b_heads):
    bsz, s, d = x.shape
    t = ROW_TILE
    nt = s // t
    n_kv = 1 + b_heads
    n_qblk = a_heads + 2 * b_heads
    use_rope = rope is not None
    kern = functools.partial(_qkv_kernel, a_heads=a_heads, a_kv_heads=a_kv_heads, b_heads=b_heads,
                             use_rope=use_rope)
    in_specs = [
        pl.BlockSpec((None, t, d), lambda b, i: (b, i, 0)),
        pl.BlockSpec((None, None, 6, d), lambda b, i: (layer, mod_row(b), 0, 0)),
        pl.BlockSpec((None, 1, d), lambda b, i: (layer, 0, 0)),
        pl.BlockSpec((None,) + w_k.shape[1:], lambda b, i: (layer, 0, 0)),
        pl.BlockSpec((None,) + w_qvt.shape[1:], lambda b, i: (layer, 0, 0)),
        pl.BlockSpec((None, HEAD_DIM, LANES), lambda b, i: (layer, 0, 0)),
        pl.BlockSpec((None, 1, LANES), lambda b, i: (layer, 0, 0)),
        pl.BlockSpec((LANES, LANES), lambda b, i: (0, 0)),
    ]
    args = [x, mods, g_attn, w_k, w_qvt, gq, gk, pm]
    if use_rope:
        in_specs += [pl.BlockSpec((t, LANES), lambda b, i: (i, 0))] * 3
        in_specs += [pl.BlockSpec((HEAD_DIM, t), lambda b, i: (0, i))] * 2
        args += list(rope)
    return pl.pallas_call(
        kern,
        out_shape=(
            jax.ShapeDtypeStruct((bsz, nt, n_qblk, LANES, t), BF16),
            jax.ShapeDtypeStruct((bsz, n_kv, s, LANES), BF16),
            jax.ShapeDtypeStruct((bsz, n_kv, V_ROWS, s), BF16),
        ),
        grid=(bsz, nt),
        in_specs=in_specs,
        out_specs=(
            pl.BlockSpec((None, None, n_qblk, LANES, t), lambda b, i: (b, i, 0, 0, 0)),
            pl.BlockSpec((None, n_kv, t, LANES), lambda b, i: (b, 0, i, 0)),
            pl.BlockSpec((None, n_kv, V_ROWS, t), lambda b, i: (b, 0, 0, i)),
        ),
        compiler_params=_cparams(),
        name="qkv_proj_x" if use_rope else "qkv_proj_c",
    )(*args)


def _attn_kernel(*refs, n_seg, a_heads, a_kv_heads, b_heads, lam_init):
    qt_ref = refs[0]
    k_refs = refs[1:1 + n_seg]
    vt_refs = refs[1 + n_seg:1 + 2 * n_seg]
    x_ref, mod_ref, wot_ref, gsub_ref, lam_ref, o_ref, ot_sc, mrg_sc = refs[1 + 2 * n_seg:]
    t = x_ref.shape[0]
    qpu = Q_BLOCKS_PER_UNIT
    n_a_units = a_heads // qpu
    n_units = n_a_units + b_heads

    chunks = []
    for k_ref, vt_ref in zip(k_refs, vt_refs):
        n_keys = k_ref.shape[1]
        ck = min(n_keys, KEY_CHUNK)
        chunks += [(k_ref, vt_ref, c0, ck) for c0 in range(0, n_keys, ck)]

    def unit_group(i, carry):
        us = [UNITS_PER_STEP * i + j for j in range(UNITS_PER_STEP)]
        kvs = [jnp.maximum(u - (n_a_units - 1), 0) for u in us]
        qts = [jnp.concatenate([qt_ref[qpu * u + j] for j in range(qpu)], axis=1) for u in us]

        def scores(j, chunk):
            k_ref, _, c0, ck = chunk
            return jnp.dot(k_ref[kvs[j], c0:c0 + ck, :], qts[j],
                           preferred_element_type=F32).astype(BF16)

        ms = [None] * UNITS_PER_STEP
        accs = [None] * UNITS_PER_STEP
        ahead = [[scores(j, ch) for ch in chunks[:SCORE_LOOKAHEAD]] for j in range(UNITS_PER_STEP)]
        for ci, (_, vt_ref, c0, ck) in enumerate(chunks):
            if ci + SCORE_LOOKAHEAD < len(chunks):
                for j in range(UNITS_PER_STEP):
                    ahead[j].append(scores(j, chunks[ci + SCORE_LOOKAHEAD]))
            for j in range(UNITS_PER_STEP):
                s = ahead[j].pop(0)
                m_c = jnp.max(s, axis=0, keepdims=True)
                m_new = m_c if ms[j] is None else jnp.maximum(ms[j], m_c)
                p = jnp.exp2(s - m_new)
                acc_c = jnp.dot(vt_ref[kvs[j], :, c0:c0 + ck], p, preferred_element_type=F32)
                if ms[j] is None:
                    accs[j] = acc_c
                else:
                    accs[j] = jnp.exp2(ms[j].astype(F32) - m_new.astype(F32)) * accs[j] + acc_c
                ms[j] = m_new
        for j, u in enumerate(us):
            ot_sc[u] = accs[j][0:LANES, :] * (1.0 / accs[j][LANES:LANES + 1, :])
        return carry

    lax.fori_loop(0, n_units // UNITS_PER_STEP, unit_group, 0)

    group = a_heads // a_kv_heads
    for hh in range(a_heads):
        kvh = hh // group
        blk = ot_sc[hh // qpu, kvh * HEAD_DIM:(kvh + 1) * HEAD_DIM, (hh % qpu) * t:(hh % qpu + 1) * t]
        mrg_sc[hh * HEAD_DIM:(hh + 1) * HEAD_DIM, :] = blk.astype(BF16)

    lv = lam_ref[...]
    lam = (jnp.exp(jnp.sum(lv[0:1, :] * lv[1:2, :], axis=-1, keepdims=True))
           - jnp.exp(jnp.sum(lv[2:3, :] * lv[3:4, :], axis=-1, keepdims=True)) + lam_init)
    gsub = _lane_tile(gsub_ref[...], t)
    a_w = a_heads * HEAD_DIM
    for hh in range(b_heads):
        o = ot_sc[n_a_units + hh]
        diff = o[:, 0:t] - lam * o[:, t:2 * t]
        sub = diff * lax.rsqrt(jnp.mean(diff * diff, axis=0, keepdims=True) + EPS) * gsub
        mrg_sc[a_w + hh * LANES:a_w + (hh + 1) * LANES, :] = (sub * (1.0 - lam_init)).astype(BF16)

    yt = jnp.dot(wot_ref[...], mrg_sc[...], preferred_element_type=F32)
    o_ref[...] = x_ref[...] + mod_ref[2:3, :] * yt.T


def _attn_call(qt, ks, vts, x, mods, w_ot, gsub, lamv, *, layer, mod_row, a_heads, a_kv_heads,
               b_heads, lam_init):
    bsz, s, d = x.shape
    t = ROW_TILE
    nt = s // t
    n_seg = len(ks)
    assert Q_BLOCKS_PER_UNIT == 2 and a_heads % (2 * a_kv_heads) == 0
    n_units = a_heads // Q_BLOCKS_PER_UNIT + b_heads
    kern = functools.partial(_attn_kernel, n_seg=n_seg, a_heads=a_heads, a_kv_heads=a_kv_heads,
                             b_heads=b_heads, lam_init=lam_init)
    once = pl.Buffered(1)
    in_specs = [pl.BlockSpec((None, None) + qt.shape[2:], lambda b, i: (b, i, 0, 0, 0))]
    in_specs += [pl.BlockSpec((None,) + k.shape[1:], lambda b, i: (b, 0, 0, 0), pipeline_mode=once)
                 for k in ks]
    in_specs += [pl.BlockSpec((None,) + v.shape[1:], lambda b, i: (b, 0, 0, 0), pipeline_mode=once)
                 for v in vts]
    in_specs += [
        pl.BlockSpec((None, t, d), lambda b, i: (b, i, 0)),
        pl.BlockSpec((None, None, 6, d), lambda b, i: (layer, mod_row(b), 0, 0)),
        pl.BlockSpec((None, d, d), lambda b, i: (layer, 0, 0), pipeline_mode=once),
        pl.BlockSpec((None, LANES, LANES), lambda b, i: (layer, 0, 0)),
        pl.BlockSpec((None, 4, HEAD_DIM), lambda b, i: (layer, 0, 0)),
    ]
    return pl.pallas_call(
        kern,
        out_shape=jax.ShapeDtypeStruct((bsz, s, d), F32),
        grid=(bsz, nt),
        in_specs=in_specs,
        out_specs=pl.BlockSpec((None, t, d), lambda b, i: (b, i, 0)),
        scratch_shapes=[
            pltpu.VMEM((n_units, LANES, Q_BLOCKS_PER_UNIT * t), F32),
            pltpu.VMEM((d, t), BF16),
        ],
        compiler_params=_cparams(),
        name="attention_x" if n_seg == 2 else "attention_c",
    )(qt, *ks, *vts, x, mods, w_ot, gsub, lamv)


def _ffn_kernel(x_ref, xp_ref, xn_ref, mod_ref, g_ref, wup_ref, cw_ref, cb_ref, wdn_ref, gfin_ref,
                o_ref, *, final_norm):
    t = x_ref.shape[0]
    i = pl.program_id(1)

    def norm_mod(x):
        return (_rms(x) * g_ref[...]) * (1.0 + mod_ref[4:5, :]) + mod_ref[3:4, :]

    x = x_ref[...]
    h_prev = jnp.where(i > 0, norm_mod(xp_ref[...]), 0.0)
    h_next = jnp.where(i < pl.num_programs(1) - 1, norm_mod(xn_ref[...]), 0.0)
    h = jnp.concatenate([h_prev, norm_mod(x), h_next], axis=0).astype(BF16)
    rows = t + 2 * HALO

    y = jnp.zeros((t, x.shape[1]), F32)
    for c in range(wup_ref.shape[0]):
        u = jnp.dot(h, wup_ref[c], preferred_element_type=F32)
        cw = cw_ref[c]
        conv = (pltpu.roll(u, 1, 0) * cw[0:1, :] + u * cw[1:2, :]
                + pltpu.roll(u, rows - 1, 0) * cw[2:3, :] + cb_ref[c])
        conv = conv[HALO:HALO + t, :]
        fc = conv.shape[1] // 2
        gate, val = conv[:, :fc], conv[:, fc:]
        hid = (gate * jax.nn.sigmoid(gate)) * val
        y = y + jnp.dot(hid.astype(BF16), wdn_ref[c], preferred_element_type=F32)

    out = x + mod_ref[5:6, :] * y
    if final_norm:
        out = _rms(out) * gfin_ref[...]
    o_ref[...] = out


def _ffn_call(x, mods, g_ffn, w_up, conv_w, conv_b, w_down, g_final, *, layer, mod_row, final_norm):
    bsz, s, d = x.shape
    t = ROW_TILE
    nt = s // t
    hpt = t // HALO
    n_halo = s // HALO
    nch, _, fc2 = w_up.shape[1:]
    kern = functools.partial(_ffn_kernel, final_norm=final_norm)
    once = pl.Buffered(1)
    return pl.pallas_call(
        kern,
        out_shape=jax.ShapeDtypeStruct((bsz, s, d), F32),
        grid=(bsz, nt),
        in_specs=[
            pl.BlockSpec((None, t, d), lambda b, i: (b, i, 0)),
            pl.BlockSpec((None, HALO, d), lambda b, i: (b, jnp.maximum(i * hpt - 1, 0), 0)),
            pl.BlockSpec((None, HALO, d),
                         lambda b, i: (b, jnp.minimum((i + 1) * hpt, n_halo - 1), 0)),
            pl.BlockSpec((None, None, 6, d), lambda b, i: (layer, mod_row(b), 0, 0)),
            pl.BlockSpec((None, 1, d), lambda b, i: (layer, 0, 0)),
            pl.BlockSpec((None, nch, d, fc2), lambda b, i: (layer, 0, 0, 0), pipeline_mode=once),
            pl.BlockSpec((None, nch, CONV_W, fc2), lambda b, i: (layer, 0, 0, 0)),
            pl.BlockSpec((None, nch, 1, fc2), lambda b, i: (layer, 0, 0, 0)),
            pl.BlockSpec((None, nch, fc2 // 2, d), lambda b, i: (layer, 0, 0, 0),
                         pipeline_mode=once),
            pl.BlockSpec((1, d), lambda b, i: (0, 0)),
        ],
        out_specs=pl.BlockSpec((None, t, d), lambda b, i: (b, i, 0)),
        compiler_params=_cparams(),
        name="conv_ffn",
    )(x, x, x, mods, g_ffn, w_up, conv_w, conv_b, w_down, g_final)


def _rope_tables(n_lat):
    rows = n_lat // GRID_W
    row = jnp.repeat(jnp.arange(rows, dtype=F32), GRID_W)
    col = jnp.tile(jnp.arange(GRID_W, dtype=F32), rows)
    inv_freq = ROPE_THETA ** (-jnp.arange(QUARTER, dtype=F32) / QUARTER)
    ang_r = row[:, None] * inv_freq[None, :]
    ang_c = col[:, None] * inv_freq[None, :]
    cos = jnp.concatenate([jnp.cos(ang_r), jnp.cos(ang_r), jnp.cos(ang_c), jnp.cos(ang_c)], axis=-1)
    sin = jnp.concatenate([jnp.sin(ang_r), jnp.sin(ang_r), jnp.sin(ang_c), jnp.sin(ang_c)], axis=-1)
    upper = (np.arange(HEAD_DIM) // QUARTER) % 2 == 1
    cos_t = cos.T
    sin_t = jnp.where(upper[None, :], sin, -sin).T
    rep = LANES // HEAD_DIM
    cos2, sin2 = jnp.tile(cos, (1, rep)), jnp.tile(sin, (1, rep))
    upper2 = np.tile(upper, rep)[None, :]
    sin_a = jnp.where(upper2, sin2, 0.0)
    sin_b = jnp.where(upper2, 0.0, -sin2)
    return cos2, sin_a, sin_b, cos_t, sin_t


def kernel(x, c, ctx, c_ctx, w_ada, b_ada, g_attn, w_in, g_qa, g_ka, lam_q1, lam_k1, lam_q2, lam_k2,
           g_sub, w_o, g_ffn, w_up, conv_w, conv_b, w_down, g_final):
    bsz, n_lat, d = x.shape
    ctx_len = ctx.shape[1]
    depth = w_ada.shape[0]
    t = ROW_TILE
    a_heads, a_kv_heads, b_heads = d // 128, d // 512, d // 256
    a_q, a_kv = a_heads * HEAD_DIM, a_kv_heads * HEAD_DIM
    b_qk, b_v = b_heads * 2 * HEAD_DIM, b_heads * 2 * HEAD_DIM
    f = w_down.shape[1]
    assert a_kv == LANES and ctx_len % t == 0 and n_lat % t == 0
    assert f % (FFN_CHUNKS * LANES) == 0

    pad = (-(bsz + 1)) % 8
    c_all = jnp.concatenate([c, c_ctx[None, :], jnp.zeros((pad, d), F32)], axis=0)
    mods = _ada_call(c_all, w_ada, b_ada).reshape(depth, bsz + 1 + pad, 6, d)
    row_x = lambda b: b
    row_c = lambda b: bsz

    o_ak, o_av, o_bq, o_bk, o_bv = np.cumsum([a_q, a_kv, a_kv, b_qk, b_qk])
    w_k = jnp.concatenate([w_in[:, :, o_ak:o_av], w_in[:, :, o_bk:o_bv]], axis=-1).astype(BF16)
    w_qvt = jnp.swapaxes(jnp.concatenate(
        [w_in[:, :, :o_ak], w_in[:, :, o_av:o_bq], w_in[:, :, o_bq:o_bk], w_in[:, :, o_bv:]],
        axis=-1), 1, 2).astype(BF16)
    w_ot = jnp.swapaxes(w_o, 1, 2).astype(BF16)

    fc = f // FFN_CHUNKS

    def chunk_cols(a):
        g_part = a[..., :f].reshape(a.shape[:-1] + (FFN_CHUNKS, fc))
        v_part = a[..., f:].reshape(a.shape[:-1] + (FFN_CHUNKS, fc))
        return jnp.concatenate([g_part, v_part], axis=-1)

    w_up_c = jnp.moveaxis(chunk_cols(w_up.astype(BF16)), 2, 1)
    conv_w_c = jnp.moveaxis(chunk_cols(conv_w), 2, 1)
    conv_b_c = chunk_cols(conv_b)[:, :, None, :]
    w_down_c = w_down.astype(BF16).reshape(depth, FFN_CHUNKS, fc, d)

    gq = jnp.broadcast_to(g_qa[:, :, None], (depth, HEAD_DIM, LANES))
    gk = jnp.tile(g_ka, (1, LANES // HEAD_DIM))[:, None, :]
    gsub = jnp.broadcast_to(g_sub[:, :, None], (depth, LANES, LANES))
    lamv = jnp.stack([lam_q1, lam_k1, lam_q2, lam_k2], axis=1)
    seg = np.arange(LANES) // HEAD_DIM
    pm = jnp.asarray((seg[:, None] == seg[None, :]) / HEAD_DIM, BF16)
    rope = _rope_tables(n_lat)
    heads = dict(a_heads=a_heads, a_kv_heads=a_kv_heads, b_heads=b_heads)

    xl, xc = x, ctx
    for l in range(depth):
        last = l == depth - 1
        lam_init = 0.8 - 0.6 * math.exp(-0.3 * l)
        qc, kc, vc = _qkv_call(xc, mods, g_attn[:, None, :], w_k, w_qvt, gq, gk, pm, None,
                               layer=l, mod_row=row_c, **heads)
        qx, kx, vx = _qkv_call(xl, mods, g_attn[:, None, :], w_k, w_qvt, gq, gk, pm, rope,
                               layer=l, mod_row=row_x, **heads)
        xl = _attn_call(qx, [kc, kx], [vc, vx], xl, mods, w_ot, gsub, lamv, layer=l, mod_row=row_x,
                        lam_init=lam_init, **heads)
        xl = _ffn_call(xl, mods, g_ffn[:, None, :], w_up_c, conv_w_c, conv_b_c, w_down_c,
                       g_final[None, :], layer=l, mod_row=row_x, final_norm=last)
        if not last:
            xc = _attn_call(qc, [kc], [vc], xc, mods, w_ot, gsub, lamv, layer=l, mod_row=row_c,
                            lam_init=lam_init, **heads)
            xc = _ffn_call(xc, mods, g_ffn[:, None, :], w_up_c, conv_w_c, conv_b_c, w_down_c,
                           g_final[None, :], layer=l, mod_row=row_c, final_norm=False)
    return xl
```

```python
import functools
import math

import jax
import jax.numpy as jnp
import numpy as np
from jax import lax
from jax.experimental import pallas as pl
from jax.experimental.pallas import tpu as pltpu

F32 = jnp.float32
BF16 = jnp.bfloat16

HEAD_DIM = 64
LANES = 128
GRID_W = 64
ROPE_THETA = 10000.0
EPS = 1e-6
CONV_W = 3
QUARTER = HEAD_DIM // 4

ROW_TILE = 256
Q_BLOCKS_PER_UNIT = 2
KEY_CHUNK = 256
V_ROWS = LANES + 16
FFN_CHUNKS = 2
HALO = 8
VMEM_LIMIT = 56 * 1024 * 1024


def _cparams():
    return pltpu.CompilerParams(
        dimension_semantics=("parallel", "arbitrary"),
        vmem_limit_bytes=VMEM_LIMIT,
    )


def _rms(x):
    return x * lax.rsqrt(jnp.mean(x * x, axis=-1, keepdims=True) + EPS)


def _lane_tile(a, width):
    return jnp.concatenate([a] * (width // LANES), axis=1)


def _ada_kernel(c_ref, w_ref, b_ref, o_ref):
    c = c_ref[...]
    s = c * jax.nn.sigmoid(c)
    o_ref[...] = jnp.dot(s.astype(BF16), w_ref[...].astype(BF16),
                         preferred_element_type=F32) + b_ref[...]


def _ada_call(c_all, w_ada, b_ada):
    depth, d, d6 = w_ada.shape
    rows = c_all.shape[0]
    tn = d6 // 4
    return pl.pallas_call(
        _ada_kernel,
        out_shape=jax.ShapeDtypeStruct((depth, rows, d6), F32),
        grid=(depth, d6 // tn),
        in_specs=[
            pl.BlockSpec((rows, d), lambda l, j: (0, 0)),
            pl.BlockSpec((None, d, tn), lambda l, j: (l, 0, j)),
            pl.BlockSpec((None, 1, tn), lambda l, j: (l, 0, j)),
        ],
        out_specs=pl.BlockSpec((None, rows, tn), lambda l, j: (l, 0, j)),
        compiler_params=_cparams(),
        name="adaln",
    )(c_all, w_ada, b_ada.reshape(depth, 1, d6))


def _qkv_kernel(*refs, a_heads, a_kv_heads, b_heads, use_rope):
    if use_rope:
        (x_ref, mod_ref, g_ref, wk_ref, wqv_ref, gq_ref, gk_ref, pm_ref,
         cos_ref, sa_ref, sb_ref, cost_ref, sint_ref, qt_ref, k_ref, vt_ref) = refs
    else:
        (x_ref, mod_ref, g_ref, wk_ref, wqv_ref, gq_ref, gk_ref, pm_ref,
         qt_ref, k_ref, vt_ref) = refs
    t = x_ref.shape[0]
    x = x_ref[...]
    h = (_rms(x) * g_ref[...]) * (1.0 + mod_ref[1:2, :]) + mod_ref[0:1, :]
    pk = jnp.dot(h.astype(BF16), wk_ref[...], preferred_element_type=F32)
    pt = jnp.dot(wqv_ref[...], h.T.astype(BF16), preferred_element_type=F32)

    def rope_k(z):
        if not use_rope:
            return z
        return (z * cos_ref[...] + pltpu.roll(z, QUARTER, 1) * sa_ref[...]
                + pltpu.roll(z, LANES - QUARTER, 1) * sb_ref[...])

    ka = pk[:, 0:LANES]
    sq = ka * ka
    hi = sq.astype(BF16)
    lo = (sq - hi.astype(F32)).astype(BF16)
    ms = (jnp.dot(hi, pm_ref[...], preferred_element_type=F32)
          + jnp.dot(lo, pm_ref[...], preferred_element_type=F32))
    k_ref[0] = rope_k(ka * lax.rsqrt(ms + EPS) * gk_ref[...]).astype(BF16)
    for hh in range(b_heads):
        k_ref[1 + hh] = rope_k(pk[:, (1 + hh) * LANES:(2 + hh) * LANES]).astype(BF16)

    def rope_q(z):
        if not use_rope:
            return z
        swap = jnp.concatenate([z[QUARTER:2 * QUARTER], z[0:QUARTER],
                                z[3 * QUARTER:4 * QUARTER], z[2 * QUARTER:3 * QUARTER]], axis=0)
        return z * cost_ref[...] + swap * sint_ref[...]

    q_scale = HEAD_DIM ** -0.5 * math.log2(math.e)
    zero = jnp.zeros((HEAD_DIM, t), BF16)
    gq = _lane_tile(gq_ref[...], t)

    def put_q(blk, half, z):
        qt_ref[blk, half * HEAD_DIM:(half + 1) * HEAD_DIM, :] = (z * q_scale).astype(BF16)
        qt_ref[blk, (1 - half) * HEAD_DIM:(2 - half) * HEAD_DIM, :] = zero

    group = a_heads // a_kv_heads
    for hh in range(a_heads):
        z = pt[hh * HEAD_DIM:(hh + 1) * HEAD_DIM, :]
        z = z * lax.rsqrt(jnp.mean(z * z, axis=0, keepdims=True) + EPS) * gq
        put_q(hh, hh // group, rope_q(z))
    off = a_heads * HEAD_DIM
    ones = jnp.ones((V_ROWS - LANES, t), BF16)
    vt_ref[0, 0:LANES, :] = pt[off:off + LANES, :].astype(BF16)
    vt_ref[0, LANES:V_ROWS, :] = ones
    off += LANES
    for hh in range(b_heads):
        for j in range(2):
            put_q(a_heads + 2 * hh + j, j, rope_q(pt[off:off + HEAD_DIM, :]))
            off += HEAD_DIM
    for hh in range(b_heads):
        vt_ref[1 + hh, 0:LANES, :] = pt[off:off + LANES, :].astype(BF16)
        vt_ref[1 + hh, LANES:V_ROWS, :] = ones
        off += LANES


def _qkv_call(x, mods, g_attn, w_k, w_qvt, gq, gk, pm, rope, *, layer, mod_row, a_heads,
              a_kv_heads, b_heads):
    bsz, s, d = x.shape
    t = ROW_TILE
    nt = s // t
    n_kv = 1 + b_heads
    n_qblk = a_heads + 2 * b_heads
    use_rope = rope is not None
    kern = functools.partial(_qkv_kernel, a_heads=a_heads, a_kv_heads=a_kv_heads, b_heads=b_heads,
                             use_rope=use_rope)
    in_specs = [
        pl.BlockSpec((None, t, d), lambda b, i: (b, i, 0)),
        pl.BlockSpec((None, None, 6, d), lambda b, i: (layer, mod_row(b), 0, 0)),
        pl.BlockSpec((None, 1, d), lambda b, i: (layer, 0, 0)),
        pl.BlockSpec((None,) + w_k.shape[1:], lambda b, i: (layer, 0, 0)),
        pl.BlockSpec((None,) + w_qvt.shape[1:], lambda b, i: (layer, 0, 0)),
        pl.BlockSpec((None, HEAD_DIM, LANES), lambda b, i: (layer, 0, 0)),
        pl.BlockSpec((None, 1, LANES), lambda b, i: (layer, 0, 0)),
        pl.BlockSpec((LANES, LANES), lambda b, i: (0, 0)),
    ]
    args = [x, mods, g_attn, w_k, w_qvt, gq, gk, pm]
    if use_rope:
        in_specs += [pl.BlockSpec((t, LANES), lambda b, i: (i, 0))] * 3
        in_specs += [pl.BlockSpec((HEAD_DIM, t), lambda b, i: (0, i))] * 2
        args += list(rope)
    return pl.pallas_call(
        kern,
        out_shape=(
            jax.ShapeDtypeStruct((bsz, nt, n_qblk, LANES, t), BF16),
            jax.ShapeDtypeStruct((bsz, n_kv, s, LANES), BF16),
            jax.ShapeDtypeStruct((bsz, n_kv, V_ROWS, s), BF16),
        ),
        grid=(bsz, nt),
        in_specs=in_specs,
        out_specs=(
            pl.BlockSpec((None, None, n_qblk, LANES, t), lambda b, i: (b, i, 0, 0, 0)),
            pl.BlockSpec((None, n_kv, t, LANES), lambda b, i: (b, 0, i, 0)),
            pl.BlockSpec((None, n_kv, V_ROWS, t), lambda b, i: (b, 0, 0, i)),
        ),
        compiler_params=_cparams(),
        name="qkv_proj_x" if use_rope else "qkv_proj_c",
    )(*args)


def _attn_kernel(*refs, n_seg, a_heads, a_kv_heads, b_heads, lam_init):
    qt_ref = refs[0]
    k_refs = refs[1:1 + n_seg]
    vt_refs = refs[1 + n_seg:1 + 2 * n_seg]
    x_ref, mod_ref, wot_ref, gsub_ref, lam_ref, o_ref, ot_sc, mrg_sc, sa_sc, sb_sc = refs[1 + 2 * n_seg:]
    t = x_ref.shape[0]
    qpu = Q_BLOCKS_PER_UNIT
    n_a_units = a_heads // qpu
    n_units = n_a_units + b_heads

    chunks, row = [], 0
    for seg, k_ref in enumerate(k_refs):
        n = k_ref.shape[1]
        ck = min(n, KEY_CHUNK)
        for c0 in range(0, n, ck):
            chunks.append((seg, c0, ck, row + c0))
        row += n

    def kv_group(u):
        return jnp.maximum(u - (n_a_units - 1), 0)

    def step(u_next, s_next, u_cur, s_cur, m_cur):
        if u_next is not None:
            kv_n = kv_group(u_next)
            qt = jnp.concatenate([qt_ref[qpu * u_next + j] for j in range(qpu)], axis=1)
        if u_cur is not None:
            kv_c = kv_group(u_cur)
        m_next = acc = None
        for seg, c0, ck, r0 in chunks:
            if u_next is not None:
                s = jnp.dot(k_refs[seg][kv_n, c0:c0 + ck, :], qt,
                            preferred_element_type=F32).astype(BF16)
                s_next[r0:r0 + ck, :] = s
                m_c = jnp.max(s, axis=0, keepdims=True)
                m_next = m_c if m_next is None else jnp.maximum(m_next, m_c)
            if u_cur is not None:
                p = jnp.exp2(s_cur[r0:r0 + ck, :] - m_cur)
                part = jnp.dot(vt_refs[seg][kv_c, :, c0:c0 + ck], p, preferred_element_type=F32)
                acc = part if acc is None else acc + part
        if u_cur is not None:
            ot_sc[u_cur] = acc[0:LANES, :] * (1.0 / acc[LANES:LANES + 1, :])
        return m_next

    def unit_pair(i, m_a):
        m_b = step(2 * i + 1, sb_sc, 2 * i, sa_sc, m_a)
        return step(2 * i + 2, sa_sc, 2 * i + 1, sb_sc, m_b)

    assert n_units % 2 == 0
    m_a = step(0, sa_sc, None, None, None)
    m_a = lax.fori_loop(0, n_units // 2 - 1, unit_pair, m_a)
    m_b = step(n_units - 1, sb_sc, n_units - 2, sa_sc, m_a)
    step(None, None, n_units - 1, sb_sc, m_b)

    group = a_heads // a_kv_heads
    for hh in range(a_heads):
        kvh = hh // group
        blk = ot_sc[hh // qpu, kvh * HEAD_DIM:(kvh + 1) * HEAD_DIM, (hh % qpu) * t:(hh % qpu + 1) * t]
        mrg_sc[hh * HEAD_DIM:(hh + 1) * HEAD_DIM, :] = blk.astype(BF16)

    lv = lam_ref[...]
    lam = (jnp.exp(jnp.sum(lv[0:1, :] * lv[1:2, :], axis=-1, keepdims=True))
           - jnp.exp(jnp.sum(lv[2:3, :] * lv[3:4, :], axis=-1, keepdims=True)) + lam_init)
    gsub = _lane_tile(gsub_ref[...], t)
    a_w = a_heads * HEAD_DIM
    for hh in range(b_heads):
        o = ot_sc[n_a_units + hh]
        diff = o[:, 0:t] - lam * o[:, t:2 * t]
        sub = diff * lax.rsqrt(jnp.mean(diff * diff, axis=0, keepdims=True) + EPS) * gsub
        mrg_sc[a_w + hh * LANES:a_w + (hh + 1) * LANES, :] = (sub * (1.0 - lam_init)).astype(BF16)

    yt = jnp.dot(wot_ref[...], mrg_sc[...], preferred_element_type=F32)
    o_ref[...] = x_ref[...] + mod_ref[2:3, :] * yt.T


def _attn_call(qt, ks, vts, x, mods, w_ot, gsub, lamv, *, layer, mod_row, a_heads, a_kv_heads,
               b_heads, lam_init):
    bsz, s, d = x.shape
    t = ROW_TILE
    nt = s // t
    n_seg = len(ks)
    assert Q_BLOCKS_PER_UNIT == 2 and a_heads % (2 * a_kv_heads) == 0
    n_units = a_heads // Q_BLOCKS_PER_UNIT + b_heads
    n_keys = sum(k.shape[2] for k in ks)
    kern = functools.partial(_attn_kernel, n_seg=n_seg, a_heads=a_heads, a_kv_heads=a_kv_heads,
                             b_heads=b_heads, lam_init=lam_init)
    once = pl.Buffered(1)
    in_specs = [pl.BlockSpec((None, None) + qt.shape[2:], lambda b, i: (b, i, 0, 0, 0))]
    in_specs += [pl.BlockSpec((None,) + k.shape[1:], lambda b, i: (b, 0, 0, 0), pipeline_mode=once)
                 for k in ks]
    in_specs += [pl.BlockSpec((None,) + v.shape[1:], lambda b, i: (b, 0, 0, 0), pipeline_mode=once)
                 for v in vts]
    in_specs += [
        pl.BlockSpec((None, t, d), lambda b, i: (b, i, 0)),
        pl.BlockSpec((None, None, 6, d), lambda b, i: (layer, mod_row(b), 0, 0)),
        pl.BlockSpec((None, d, d), lambda b, i: (layer, 0, 0), pipeline_mode=once),
        pl.BlockSpec((None, LANES, LANES), lambda b, i: (layer, 0, 0)),
        pl.BlockSpec((None, 4, HEAD_DIM), lambda b, i: (layer, 0, 0)),
    ]
    return pl.pallas_call(
        kern,
        out_shape=jax.ShapeDtypeStruct((bsz, s, d), F32),
        grid=(bsz, nt),
        in_specs=in_specs,
        out_specs=pl.BlockSpec((None, t, d), lambda b, i: (b, i, 0)),
        scratch_shapes=[
            pltpu.VMEM((n_units, LANES, Q_BLOCKS_PER_UNIT * t), F32),
            pltpu.VMEM((d, t), BF16),
        ] + [pltpu.VMEM((n_keys, Q_BLOCKS_PER_UNIT * t), BF16)] * 2,
        compiler_params=_cparams(),
        name="attention_x" if n_seg == 2 else "attention_c",
    )(qt, *ks, *vts, x, mods, w_ot, gsub, lamv)


def _ffn_kernel(x_ref, xp_ref, xn_ref, mod_ref, g_ref, wup_ref, cw_ref, cb_ref, wdn_ref, gfin_ref,
                o_ref, *, final_norm):
    t = x_ref.shape[0]
    i = pl.program_id(1)

    def norm_mod(x):
        return (_rms(x) * g_ref[...]) * (1.0 + mod_ref[4:5, :]) + mod_ref[3:4, :]

    x = x_ref[...]
    h_prev = jnp.where(i > 0, norm_mod(xp_ref[...]), 0.0)
    h_next = jnp.where(i < pl.num_programs(1) - 1, norm_mod(xn_ref[...]), 0.0)
    h = jnp.concatenate([h_prev, norm_mod(x), h_next], axis=0).astype(BF16)
    rows = t + 2 * HALO

    y = jnp.zeros((t, x.shape[1]), F32)
    for c in range(wup_ref.shape[0]):
        u = jnp.dot(h, wup_ref[c], preferred_element_type=F32)
        cw = cw_ref[c]
        conv = (pltpu.roll(u, 1, 0) * cw[0:1, :] + u * cw[1:2, :]
                + pltpu.roll(u, rows - 1, 0) * cw[2:3, :] + cb_ref[c])
        conv = conv[HALO:HALO + t, :]
        fc = conv.shape[1] // 2
        gate, val = conv[:, :fc], conv[:, fc:]
        hid = (gate * jax.nn.sigmoid(gate)) * val
        y = y + jnp.dot(hid.astype(BF16), wdn_ref[c], preferred_element_type=F32)

    out = x + mod_ref[5:6, :] * y
    if final_norm:
        out = _rms(out) * gfin_ref[...]
    o_ref[...] = out


def _ffn_call(x, mods, g_ffn, w_up, conv_w, conv_b, w_down, g_final, *, layer, mod_row, final_norm):
    bsz, s, d = x.shape
    t = ROW_TILE
    nt = s // t
    hpt = t // HALO
    n_halo = s // HALO
    nch, _, fc2 = w_up.shape[1:]
    kern = functools.partial(_ffn_kernel, final_norm=final_norm)
    once = pl.Buffered(1)
    return pl.pallas_call(
        kern,
        out_shape=jax.ShapeDtypeStruct((bsz, s, d), F32),
        grid=(bsz, nt),
        in_specs=[
            pl.BlockSpec((None, t, d), lambda b, i: (b, i, 0)),
            pl.BlockSpec((None, HALO, d), lambda b, i: (b, jnp.maximum(i * hpt - 1, 0), 0)),
            pl.BlockSpec((None, HALO, d),
                         lambda b, i: (b, jnp.minimum((i + 1) * hpt, n_halo - 1), 0)),
            pl.BlockSpec((None, None, 6, d), lambda b, i: (layer, mod_row(b), 0, 0)),
            pl.BlockSpec((None, 1, d), lambda b, i: (layer, 0, 0)),
            pl.BlockSpec((None, nch, d, fc2), lambda b, i: (layer, 0, 0, 0), pipeline_mode=once),
            pl.BlockSpec((None, nch, CONV_W, fc2), lambda b, i: (layer, 0, 0, 0)),
            pl.BlockSpec((None, nch, 1, fc2), lambda b, i: (layer, 0, 0, 0)),
            pl.BlockSpec((None, nch, fc2 // 2, d), lambda b, i: (layer, 0, 0, 0),
                         pipeline_mode=once),
            pl.BlockSpec((1, d), lambda b, i: (0, 0)),
        ],
        out_specs=pl.BlockSpec((None, t, d), lambda b, i: (b, i, 0)),
        compiler_params=_cparams(),
        name="conv_ffn",
    )(x, x, x, mods, g_ffn, w_up, conv_w, conv_b, w_down, g_final)


def _rope_tables(n_lat):
    rows = n_lat // GRID_W
    row = jnp.repeat(jnp.arange(rows, dtype=F32), GRID_W)
    col = jnp.tile(jnp.arange(GRID_W, dtype=F32), rows)
    inv_freq = ROPE_THETA ** (-jnp.arange(QUARTER, dtype=F32) / QUARTER)
    ang_r = row[:, None] * inv_freq[None, :]
    ang_c = col[:, None] * inv_freq[None, :]
    cos = jnp.concatenate([jnp.cos(ang_r), jnp.cos(ang_r), jnp.cos(ang_c), jnp.cos(ang_c)], axis=-1)
    sin = jnp.concatenate([jnp.sin(ang_r), jnp.sin(ang_r), jnp.sin(ang_c), jnp.sin(ang_c)], axis=-1)
    upper = (np.arange(HEAD_DIM) // QUARTER) % 2 == 1
    cos_t = cos.T
    sin_t = jnp.where(upper[None, :], sin, -sin).T
    rep = LANES // HEAD_DIM
    cos2, sin2 = jnp.tile(cos, (1, rep)), jnp.tile(sin, (1, rep))
    upper2 = np.tile(upper, rep)[None, :]
    sin_a = jnp.where(upper2, sin2, 0.0)
    sin_b = jnp.where(upper2, 0.0, -sin2)
    return cos2, sin_a, sin_b, cos_t, sin_t


def kernel(x, c, ctx, c_ctx, w_ada, b_ada, g_attn, w_in, g_qa, g_ka, lam_q1, lam_k1, lam_q2, lam_k2,
           g_sub, w_o, g_ffn, w_up, conv_w, conv_b, w_down, g_final):
    bsz, n_lat, d = x.shape
    ctx_len = ctx.shape[1]
    depth = w_ada.shape[0]
    t = ROW_TILE
    a_heads, a_kv_heads, b_heads = d // 128, d // 512, d // 256
    a_q, a_kv = a_heads * HEAD_DIM, a_kv_heads * HEAD_DIM
    b_qk, b_v = b_heads * 2 * HEAD_DIM, b_heads * 2 * HEAD_DIM
    f = w_down.shape[1]
    assert a_kv == LANES and ctx_len % t == 0 and n_lat % t == 0
    assert f % (FFN_CHUNKS * LANES) == 0

    pad = (-(bsz + 1)) % 8
    c_all = jnp.concatenate([c, c_ctx[None, :], jnp.zeros((pad, d), F32)], axis=0)
    mods = _ada_call(c_all, w_ada, b_ada).reshape(depth, bsz + 1 + pad, 6, d)
    row_x = lambda b: b
    row_c = lambda b: bsz

    o_ak, o_av, o_bq, o_bk, o_bv = np.cumsum([a_q, a_kv, a_kv, b_qk, b_qk])
    w_k = jnp.concatenate([w_in[:, :, o_ak:o_av], w_in[:, :, o_bk:o_bv]], axis=-1).astype(BF16)
    w_qvt = jnp.swapaxes(jnp.concatenate(
        [w_in[:, :, :o_ak], w_in[:, :, o_av:o_bq], w_in[:, :, o_bq:o_bk], w_in[:, :, o_bv:]],
        axis=-1), 1, 2).astype(BF16)
    w_ot = jnp.swapaxes(w_o, 1, 2).astype(BF16)

    fc = f // FFN_CHUNKS

    def chunk_cols(a):
        g_part = a[..., :f].reshape(a.shape[:-1] + (FFN_CHUNKS, fc))
        v_part = a[..., f:].reshape(a.shape[:-1] + (FFN_CHUNKS, fc))
        return jnp.concatenate([g_part, v_part], axis=-1)

    w_up_c = jnp.moveaxis(chunk_cols(w_up.astype(BF16)), 2, 1)
    conv_w_c = jnp.moveaxis(chunk_cols(conv_w), 2, 1)
    conv_b_c = chunk_cols(conv_b)[:, :, None, :]
    w_down_c = w_down.astype(BF16).reshape(depth, FFN_CHUNKS, fc, d)

    gq = jnp.broadcast_to(g_qa[:, :, None], (depth, HEAD_DIM, LANES))
    gk = jnp.tile(g_ka, (1, LANES // HEAD_DIM))[:, None, :]
    gsub = jnp.broadcast_to(g_sub[:, :, None], (depth, LANES, LANES))
    lamv = jnp.stack([lam_q1, lam_k1, lam_q2, lam_k2], axis=1)
    seg = np.arange(LANES) // HEAD_DIM
    pm = jnp.asarray((seg[:, None] == seg[None, :]) / HEAD_DIM, BF16)
    rope = _rope_tables(n_lat)
    heads = dict(a_heads=a_heads, a_kv_heads=a_kv_heads, b_heads=b_heads)

    xl, xc = x, ctx
    for l in range(depth):
        last = l == depth - 1
        lam_init = 0.8 - 0.6 * math.exp(-0.3 * l)
        qc, kc, vc = _qkv_call(xc, mods, g_attn[:, None, :], w_k, w_qvt, gq, gk, pm, None,
                               layer=l, mod_row=row_c, **heads)
        qx, kx, vx = _qkv_call(xl, mods, g_attn[:, None, :], w_k, w_qvt, gq, gk, pm, rope,
                               layer=l, mod_row=row_x, **heads)
        xl = _attn_call(qx, [kc, kx], [vc, vx], xl, mods, w_ot, gsub, lamv, layer=l, mod_row=row_x,
                        lam_init=lam_init, **heads)
        xl = _ffn_call(xl, mods, g_ffn[:, None, :], w_up_c, conv_w_c, conv_b_c, w_down_c,
                       g_final[None, :], layer=l, mod_row=row_x, final_norm=last)
        if not last:
            xc = _attn_call(qc, [kc], [vc], xc, mods, w_ot, gsub, lamv, layer=l, mod_row=row_c,
                            lam_init=lam_init, **heads)
            xc = _ffn_call(xc, mods, g_ffn[:, None, :], w_up_c, conv_w_c, conv_b_c, w_down_c,
                           g_final[None, :], layer=l, mod_row=row_c, final_norm=False)
    return xl
```

```python
import functools
import math

import jax
import jax.numpy as jnp
import numpy as np
from jax import lax
from jax.experimental import pallas as pl
from jax.experimental.pallas import tpu as pltpu

F32 = jnp.float32
BF16 = jnp.bfloat16

HEAD_DIM = 64
LANES = 128
GRID_W = 64
ROPE_THETA = 10000.0
EPS = 1e-6
CONV_W = 3
QUARTER = HEAD_DIM // 4

ROW_TILE = 512
UNIT_COLS = 512
KEY_CHUNK = 256
V_ROWS = LANES + 16
FFN_CHUNKS = 1
HALO = 8
VMEM_LIMIT = 56 * 1024 * 1024


def _cparams():
    return pltpu.CompilerParams(
        dimension_semantics=("parallel", "arbitrary"),
        vmem_limit_bytes=VMEM_LIMIT,
    )


def _rms(x):
    return x * lax.rsqrt(jnp.mean(x * x, axis=-1, keepdims=True) + EPS)


def _lane_tile(a, width):
    return jnp.concatenate([a] * (width // LANES), axis=1)


def _ada_kernel(c_ref, w_ref, b_ref, o_ref):
    c = c_ref[...]
    s = c * jax.nn.sigmoid(c)
    o_ref[...] = jnp.dot(s.astype(BF16), w_ref[...].astype(BF16),
                         preferred_element_type=F32) + b_ref[...]


def _ada_call(c_all, w_ada, b_ada):
    depth, d, d6 = w_ada.shape
    rows = c_all.shape[0]
    tn = d6 // 4
    return pl.pallas_call(
        _ada_kernel,
        out_shape=jax.ShapeDtypeStruct((depth, rows, d6), F32),
        grid=(depth, d6 // tn),
        in_specs=[
            pl.BlockSpec((rows, d), lambda l, j: (0, 0)),
            pl.BlockSpec((None, d, tn), lambda l, j: (l, 0, j)),
            pl.BlockSpec((None, 1, tn), lambda l, j: (l, 0, j)),
        ],
        out_specs=pl.BlockSpec((None, rows, tn), lambda l, j: (l, 0, j)),
        compiler_params=_cparams(),
        name="adaln",
    )(c_all, w_ada, b_ada.reshape(depth, 1, d6))


def _qkv_kernel(*refs, a_heads, a_kv_heads, b_heads, use_rope):
    if use_rope:
        (x_ref, mod_ref, g_ref, wk_ref, wqv_ref, gq_ref, gk_ref, pm_ref,
         cos_ref, sa_ref, sb_ref, cost_ref, sint_ref, qt_ref, k_ref, vt_ref) = refs
    else:
        (x_ref, mod_ref, g_ref, wk_ref, wqv_ref, gq_ref, gk_ref, pm_ref,
         qt_ref, k_ref, vt_ref) = refs
    t = x_ref.shape[0]
    x = x_ref[...]
    h = (_rms(x) * g_ref[...]) * (1.0 + mod_ref[1:2, :]) + mod_ref[0:1, :]
    pk = jnp.dot(h.astype(BF16), wk_ref[...], preferred_element_type=F32)
    pt = jnp.dot(wqv_ref[...], h.T.astype(BF16), preferred_element_type=F32)

    def rope_k(z):
        if not use_rope:
            return z
        return (z * cos_ref[...] + pltpu.roll(z, QUARTER, 1) * sa_ref[...]
                + pltpu.roll(z, LANES - QUARTER, 1) * sb_ref[...])

    ka = pk[:, 0:LANES]
    sq = ka * ka
    hi = sq.astype(BF16)
    lo = (sq - hi.astype(F32)).astype(BF16)
    ms = (jnp.dot(hi, pm_ref[...], preferred_element_type=F32)
          + jnp.dot(lo, pm_ref[...], preferred_element_type=F32))
    k_ref[0] = rope_k(ka * lax.rsqrt(ms + EPS) * gk_ref[...]).astype(BF16)
    for hh in range(b_heads):
        k_ref[1 + hh] = rope_k(pk[:, (1 + hh) * LANES:(2 + hh) * LANES]).astype(BF16)

    def rope_q(z):
        if not use_rope:
            return z
        swap = jnp.concatenate([z[QUARTER:2 * QUARTER], z[0:QUARTER],
                                z[3 * QUARTER:4 * QUARTER], z[2 * QUARTER:3 * QUARTER]], axis=0)
        return z * cost_ref[...] + swap * sint_ref[...]

    q_scale = HEAD_DIM ** -0.5 * math.log2(math.e)
    zero = jnp.zeros((HEAD_DIM, t), BF16)
    gq = _lane_tile(gq_ref[...], t)

    def put_q(blk, half, z):
        qt_ref[blk, half * HEAD_DIM:(half + 1) * HEAD_DIM, :] = (z * q_scale).astype(BF16)
        qt_ref[blk, (1 - half) * HEAD_DIM:(2 - half) * HEAD_DIM, :] = zero

    group = a_heads // a_kv_heads
    for hh in range(a_heads):
        z = pt[hh * HEAD_DIM:(hh + 1) * HEAD_DIM, :]
        z = z * lax.rsqrt(jnp.mean(z * z, axis=0, keepdims=True) + EPS) * gq
        put_q(hh, hh // group, rope_q(z))
    off = a_heads * HEAD_DIM
    ones = jnp.ones((V_ROWS - LANES, t), BF16)
    vt_ref[0, 0:LANES, :] = pt[off:off + LANES, :].astype(BF16)
    vt_ref[0, LANES:V_ROWS, :] = ones
    off += LANES
    for hh in range(b_heads):
        for j in range(2):
            put_q(a_heads + 2 * hh + j, j, rope_q(pt[off:off + HEAD_DIM, :]))
            off += HEAD_DIM
    for hh in range(b_heads):
        vt_ref[1 + hh, 0:LANES, :] = pt[off:off + LANES, :].astype(BF16)
        vt_ref[1 + hh, LANES:V_ROWS, :] = ones
        off += LANES


def _qkv_call(x, mods, g_attn, w_k, w_qvt, gq, gk, pm, rope, *, layer, mod_row, a_heads,
              a_kv_heads, b_heads):
    bsz, s, d = x.shape
    t = min(ROW_TILE, s)
    nt = s // t
    n_kv = 1 + b_heads
    n_qblk = a_heads + 2 * b_heads
    use_rope = rope is not None
    kern = functools.partial(_qkv_kernel, a_heads=a_heads, a_kv_heads=a_kv_heads, b_heads=b_heads,
                             use_rope=use_rope)
    in_specs = [
        pl.BlockSpec((None, t, d), lambda b, i: (b, i, 0)),
        pl.BlockSpec((None, None, 6, d), lambda b, i: (layer, mod_row(b), 0, 0)),
        pl.BlockSpec((None, 1, d), lambda b, i: (layer, 0, 0)),
        pl.BlockSpec((None,) + w_k.shape[1:], lambda b, i: (layer, 0, 0)),
        pl.BlockSpec((None,) + w_qvt.shape[1:], lambda b, i: (layer, 0, 0)),
        pl.BlockSpec((None, HEAD_DIM, LANES), lambda b, i: (layer, 0, 0)),
        pl.BlockSpec((None, 1, LANES), lambda b, i: (layer, 0, 0)),
        pl.BlockSpec((LANES, LANES), lambda b, i: (0, 0)),
    ]
    args = [x, mods, g_attn, w_k, w_qvt, gq, gk, pm]
    if use_rope:
        in_specs += [pl.BlockSpec((t, LANES), lambda b, i: (i, 0))] * 3
        in_specs += [pl.BlockSpec((HEAD_DIM, t), lambda b, i: (0, i))] * 2
        args += list(rope)
    return pl.pallas_call(
        kern,
        out_shape=(
            jax.ShapeDtypeStruct((bsz, nt, n_qblk, LANES, t), BF16),
            jax.ShapeDtypeStruct((bsz, n_kv, s, LANES), BF16),
            jax.ShapeDtypeStruct((bsz, n_kv, V_ROWS, s), BF16),
        ),
        grid=(bsz, nt),
        in_specs=in_specs,
        out_specs=(
            pl.BlockSpec((None, None, n_qblk, LANES, t), lambda b, i: (b, i, 0, 0, 0)),
            pl.BlockSpec((None, n_kv, t, LANES), lambda b, i: (b, 0, i, 0)),
            pl.BlockSpec((None, n_kv, V_ROWS, t), lambda b, i: (b, 0, 0, i)),
        ),
        compiler_params=_cparams(),
        name="qkv_proj_x" if use_rope else "qkv_proj_c",
    )(*args)


def _attn_kernel(*refs, n_seg, a_heads, a_kv_heads, b_heads, lam_init):
    qt_ref = refs[0]
    k_refs = refs[1:1 + n_seg]
    vt_refs = refs[1 + n_seg:1 + 2 * n_seg]
    x_ref, mod_ref, wot_ref, gsub_ref, lam_ref, o_ref, ot_sc, mrg_sc, sa_sc, sb_sc = refs[1 + 2 * n_seg:]
    t = x_ref.shape[0]
    qpu = UNIT_COLS // t
    n_units = qt_ref.shape[0] // qpu

    chunks, row = [], 0
    for seg, k_ref in enumerate(k_refs):
        n = k_ref.shape[1]
        ck = min(n, KEY_CHUNK)
        for c0 in range(0, n, ck):
            chunks.append((seg, c0, ck, row + c0))
        row += n

    def kv_group(u):
        return jnp.maximum(lax.shift_right_arithmetic(u * qpu - a_heads, 1) + 1, 0)

    def block_out(blk):
        return ot_sc.at[blk // qpu, :, (blk % qpu) * t:(blk % qpu + 1) * t]

    def step(u_next, s_next, u_cur, s_cur, m_cur):
        if u_next is not None:
            kv_n = kv_group(u_next)
            qt = jnp.concatenate([qt_ref[qpu * u_next + j] for j in range(qpu)], axis=1)
        if u_cur is not None:
            kv_c = kv_group(u_cur)
        m_next = acc = None
        for seg, c0, ck, r0 in chunks:
            if u_next is not None:
                s = jnp.dot(k_refs[seg][kv_n, c0:c0 + ck, :], qt,
                            preferred_element_type=F32).astype(BF16)
                s_next[r0:r0 + ck, :] = s
                m_c = jnp.max(s, axis=0, keepdims=True)
                m_next = m_c if m_next is None else jnp.maximum(m_next, m_c)
            if u_cur is not None:
                p = jnp.exp2(s_cur[r0:r0 + ck, :] - m_cur)
                part = jnp.dot(vt_refs[seg][kv_c, :, c0:c0 + ck], p, preferred_element_type=F32)
                acc = part if acc is None else acc + part
        if u_cur is not None:
            ot_sc[u_cur] = acc[0:LANES, :] * (1.0 / acc[LANES:LANES + 1, :])
        return m_next

    def unit_pair(i, m_a):
        m_b = step(2 * i + 1, sb_sc, 2 * i, sa_sc, m_a)
        return step(2 * i + 2, sa_sc, 2 * i + 1, sb_sc, m_b)

    assert n_units % 2 == 0
    m_a = step(0, sa_sc, None, None, None)
    m_a = lax.fori_loop(0, n_units // 2 - 1, unit_pair, m_a)
    m_b = step(n_units - 1, sb_sc, n_units - 2, sa_sc, m_a)
    step(None, None, n_units - 1, sb_sc, m_b)

    group = a_heads // a_kv_heads
    for hh in range(a_heads):
        kvh = hh // group
        blk = block_out(hh)[kvh * HEAD_DIM:(kvh + 1) * HEAD_DIM, :]
        mrg_sc[hh * HEAD_DIM:(hh + 1) * HEAD_DIM, :] = blk.astype(BF16)

    lv = lam_ref[...]
    lam = (jnp.exp(jnp.sum(lv[0:1, :] * lv[1:2, :], axis=-1, keepdims=True))
           - jnp.exp(jnp.sum(lv[2:3, :] * lv[3:4, :], axis=-1, keepdims=True)) + lam_init)
    gsub = _lane_tile(gsub_ref[...], t)
    a_w = a_heads * HEAD_DIM
    for hh in range(b_heads):
        diff = block_out(a_heads + 2 * hh)[...] - lam * block_out(a_heads + 2 * hh + 1)[...]
        sub = diff * lax.rsqrt(jnp.mean(diff * diff, axis=0, keepdims=True) + EPS) * gsub
        mrg_sc[a_w + hh * LANES:a_w + (hh + 1) * LANES, :] = (sub * (1.0 - lam_init)).astype(BF16)

    yt = jnp.dot(wot_ref[...], mrg_sc[...], preferred_element_type=F32)
    o_ref[...] = x_ref[...] + mod_ref[2:3, :] * yt.T


def _attn_call(qt, ks, vts, x, mods, w_ot, gsub, lamv, *, layer, mod_row, a_heads, a_kv_heads,
               b_heads, lam_init):
    bsz, s, d = x.shape
    t = min(ROW_TILE, s)
    nt = s // t
    n_seg = len(ks)
    assert UNIT_COLS % t == 0 and a_heads % (UNIT_COLS // t) == 0
    n_units = qt.shape[2] * t // UNIT_COLS
    n_keys = sum(k.shape[2] for k in ks)
    kern = functools.partial(_attn_kernel, n_seg=n_seg, a_heads=a_heads, a_kv_heads=a_kv_heads,
                             b_heads=b_heads, lam_init=lam_init)
    once = pl.Buffered(1)
    in_specs = [pl.BlockSpec((None, None) + qt.shape[2:], lambda b, i: (b, i, 0, 0, 0))]
    in_specs += [pl.BlockSpec((None,) + k.shape[1:], lambda b, i: (b, 0, 0, 0), pipeline_mode=once)
                 for k in ks]
    in_specs += [pl.BlockSpec((None,) + v.shape[1:], lambda b, i: (b, 0, 0, 0), pipeline_mode=once)
                 for v in vts]
    in_specs += [
        pl.BlockSpec((None, t, d), lambda b, i: (b, i, 0)),
        pl.BlockSpec((None, None, 6, d), lambda b, i: (layer, mod_row(b), 0, 0)),
        pl.BlockSpec((None, d, d), lambda b, i: (layer, 0, 0), pipeline_mode=once),
        pl.BlockSpec((None, LANES, LANES), lambda b, i: (layer, 0, 0)),
        pl.BlockSpec((None, 4, HEAD_DIM), lambda b, i: (layer, 0, 0)),
    ]
    return pl.pallas_call(
        kern,
        out_shape=jax.ShapeDtypeStruct((bsz, s, d), F32),
        grid=(bsz, nt),
        in_specs=in_specs,
        out_specs=pl.BlockSpec((None, t, d), lambda b, i: (b, i, 0)),
        scratch_shapes=[
            pltpu.VMEM((n_units, LANES, UNIT_COLS), F32),
            pltpu.VMEM((d, t), BF16),
        ] + [pltpu.VMEM((n_keys, UNIT_COLS), BF16)] * 2,
        compiler_params=_cparams(),
        name="attention_x" if n_seg == 2 else "attention_c",
    )(qt, *ks, *vts, x, mods, w_ot, gsub, lamv)


def _ffn_kernel(x_ref, xp_ref, xn_ref, mod_ref, g_ref, wup_ref, cw_ref, cb_ref, wdn_ref, gfin_ref,
                o_ref, *, final_norm):
    t = x_ref.shape[0]
    i = pl.program_id(1)

    def norm_mod(x):
        return (_rms(x) * g_ref[...]) * (1.0 + mod_ref[4:5, :]) + mod_ref[3:4, :]

    x = x_ref[...]
    h_prev = jnp.where(i > 0, norm_mod(xp_ref[...]), 0.0)
    h_next = jnp.where(i < pl.num_programs(1) - 1, norm_mod(xn_ref[...]), 0.0)
    h = jnp.concatenate([h_prev, norm_mod(x), h_next], axis=0).astype(BF16)
    rows = t + 2 * HALO

    y = jnp.zeros((t, x.shape[1]), F32)
    for c in range(wup_ref.shape[0]):
        u = jnp.dot(h, wup_ref[c], preferred_element_type=F32)
        cw = cw_ref[c]
        conv = (pltpu.roll(u, 1, 0) * cw[0:1, :] + u * cw[1:2, :]
                + pltpu.roll(u, rows - 1, 0) * cw[2:3, :] + cb_ref[c])
        conv = conv[HALO:HALO + t, :]
        fc = conv.shape[1] // 2
        gate, val = conv[:, :fc], conv[:, fc:]
        hid = (gate * jax.nn.sigmoid(gate)) * val
        y = y + jnp.dot(hid.astype(BF16), wdn_ref[c], preferred_element_type=F32)

    out = x + mod_ref[5:6, :] * y
    if final_norm:
        out = _rms(out) * gfin_ref[...]
    o_ref[...] = out


def _ffn_call(x, mods, g_ffn, w_up, conv_w, conv_b, w_down, g_final, *, layer, mod_row, final_norm):
    bsz, s, d = x.shape
    t = min(ROW_TILE, s)
    nt = s // t
    hpt = t // HALO
    n_halo = s // HALO
    nch, _, fc2 = w_up.shape[1:]
    kern = functools.partial(_ffn_kernel, final_norm=final_norm)
    once = pl.Buffered(1)
    return pl.pallas_call(
        kern,
        out_shape=jax.ShapeDtypeStruct((bsz, s, d), F32),
        grid=(bsz, nt),
        in_specs=[
            pl.BlockSpec((None, t, d), lambda b, i: (b, i, 0)),
            pl.BlockSpec((None, HALO, d), lambda b, i: (b, jnp.maximum(i * hpt - 1, 0), 0)),
            pl.BlockSpec((None, HALO, d),
                         lambda b, i: (b, jnp.minimum((i + 1) * hpt, n_halo - 1), 0)),
            pl.BlockSpec((None, None, 6, d), lambda b, i: (layer, mod_row(b), 0, 0)),
            pl.BlockSpec((None, 1, d), lambda b, i: (layer, 0, 0)),
            pl.BlockSpec((None, nch, d, fc2), lambda b, i: (layer, 0, 0, 0), pipeline_mode=once),
            pl.BlockSpec((None, nch, CONV_W, fc2), lambda b, i: (layer, 0, 0, 0)),
            pl.BlockSpec((None, nch, 1, fc2), lambda b, i: (layer, 0, 0, 0)),
            pl.BlockSpec((None, nch, fc2 // 2, d), lambda b, i: (layer, 0, 0, 0),
                         pipeline_mode=once),
            pl.BlockSpec((1, d), lambda b, i: (0, 0)),
        ],
        out_specs=pl.BlockSpec((None, t, d), lambda b, i: (b, i, 0)),
        compiler_params=_cparams(),
        name="conv_ffn",
    )(x, x, x, mods, g_ffn, w_up, conv_w, conv_b, w_down, g_final)


def _rope_tables(n_lat):
    rows = n_lat // GRID_W
    row = jnp.repeat(jnp.arange(rows, dtype=F32), GRID_W)
    col = jnp.tile(jnp.arange(GRID_W, dtype=F32), rows)
    inv_freq = ROPE_THETA ** (-jnp.arange(QUARTER, dtype=F32) / QUARTER)
    ang_r = row[:, None] * inv_freq[None, :]
    ang_c = col[:, None] * inv_freq[None, :]
    cos = jnp.concatenate([jnp.cos(ang_r), jnp.cos(ang_r), jnp.cos(ang_c), jnp.cos(ang_c)], axis=-1)
    sin = jnp.concatenate([jnp.sin(ang_r), jnp.sin(ang_r), jnp.sin(ang_c), jnp.sin(ang_c)], axis=-1)
    upper = (np.arange(HEAD_DIM) // QUARTER) % 2 == 1
    cos_t = cos.T
    sin_t = jnp.where(upper[None, :], sin, -sin).T
    rep = LANES // HEAD_DIM
    cos2, sin2 = jnp.tile(cos, (1, rep)), jnp.tile(sin, (1, rep))
    upper2 = np.tile(upper, rep)[None, :]
    sin_a = jnp.where(upper2, sin2, 0.0)
    sin_b = jnp.where(upper2, 0.0, -sin2)
    return cos2, sin_a, sin_b, cos_t, sin_t


def kernel(x, c, ctx, c_ctx, w_ada, b_ada, g_attn, w_in, g_qa, g_ka, lam_q1, lam_k1, lam_q2, lam_k2,
           g_sub, w_o, g_ffn, w_up, conv_w, conv_b, w_down, g_final):
    bsz, n_lat, d = x.shape
    ctx_len = ctx.shape[1]
    depth = w_ada.shape[0]
    a_heads, a_kv_heads, b_heads = d // 128, d // 512, d // 256
    a_q, a_kv = a_heads * HEAD_DIM, a_kv_heads * HEAD_DIM
    b_qk, b_v = b_heads * 2 * HEAD_DIM, b_heads * 2 * HEAD_DIM
    f = w_down.shape[1]
    assert a_kv == LANES
    assert all(n % min(ROW_TILE, n) == 0 and UNIT_COLS % min(ROW_TILE, n) == 0 for n in (ctx_len, n_lat))
    assert f % (FFN_CHUNKS * LANES) == 0

    pad = (-(bsz + 1)) % 8
    c_all = jnp.concatenate([c, c_ctx[None, :], jnp.zeros((pad, d), F32)], axis=0)
    mods = _ada_call(c_all, w_ada, b_ada).reshape(depth, bsz + 1 + pad, 6, d)
    row_x = lambda b: b
    row_c = lambda b: bsz

    o_ak, o_av, o_bq, o_bk, o_bv = np.cumsum([a_q, a_kv, a_kv, b_qk, b_qk])
    w_k = jnp.concatenate([w_in[:, :, o_ak:o_av], w_in[:, :, o_bk:o_bv]], axis=-1).astype(BF16)
    w_qvt = jnp.swapaxes(jnp.concatenate(
        [w_in[:, :, :o_ak], w_in[:, :, o_av:o_bq], w_in[:, :, o_bq:o_bk], w_in[:, :, o_bv:]],
        axis=-1), 1, 2).astype(BF16)
    w_ot = jnp.swapaxes(w_o, 1, 2).astype(BF16)

    fc = f // FFN_CHUNKS

    def chunk_cols(a):
        g_part = a[..., :f].reshape(a.shape[:-1] + (FFN_CHUNKS, fc))
        v_part = a[..., f:].reshape(a.shape[:-1] + (FFN_CHUNKS, fc))
        return jnp.concatenate([g_part, v_part], axis=-1)

    w_up_c = jnp.moveaxis(chunk_cols(w_up.astype(BF16)), 2, 1)
    conv_w_c = jnp.moveaxis(chunk_cols(conv_w), 2, 1)
    conv_b_c = chunk_cols(conv_b)[:, :, None, :]
    w_down_c = w_down.astype(BF16).reshape(depth, FFN_CHUNKS, fc, d)

    gq = jnp.broadcast_to(g_qa[:, :, None], (depth, HEAD_DIM, LANES))
    gk = jnp.tile(g_ka, (1, LANES // HEAD_DIM))[:, None, :]
    gsub = jnp.broadcast_to(g_sub[:, :, None], (depth, LANES, LANES))
    lamv = jnp.stack([lam_q1, lam_k1, lam_q2, lam_k2], axis=1)
    seg = np.arange(LANES) // HEAD_DIM
    pm = jnp.asarray((seg[:, None] == seg[None, :]) / HEAD_DIM, BF16)
    rope = _rope_tables(n_lat)
    heads = dict(a_heads=a_heads, a_kv_heads=a_kv_heads, b_heads=b_heads)

    xl, xc = x, ctx
    for l in range(depth):
        last = l == depth - 1
        lam_init = 0.8 - 0.6 * math.exp(-0.3 * l)
        qc, kc, vc = _qkv_call(xc, mods, g_attn[:, None, :], w_k, w_qvt, gq, gk, pm, None,
                               layer=l, mod_row=row_c, **heads)
        qx, kx, vx = _qkv_call(xl, mods, g_attn[:, None, :], w_k, w_qvt, gq, gk, pm, rope,
                               layer=l, mod_row=row_x, **heads)
        xl = _attn_call(qx, [kc, kx], [vc, vx], xl, mods, w_ot, gsub, lamv, layer=l, mod_row=row_x,
                        lam_init=lam_init, **heads)
        xl = _ffn_call(xl, mods, g_ffn[:, None, :], w_up_c, conv_w_c, conv_b_c, w_down_c,
                       g_final[None, :], layer=l, mod_row=row_x, final_norm=last)
        if not last:
            xc = _attn_call(qc, [kc], [vc], xc, mods, w_ot, gsub, lamv, layer=l, mod_row=row_c,
                            lam_init=lam_init, **heads)
            xc = _ffn_call(xc, mods, g_ffn[:, None, :], w_up_c, conv_w_c, conv_b_c, w_down_c,
                           g_final[None, :], layer=l, mod_row=row_c, final_norm=False)
    return xl
```

```python
import functools
import math

import jax
import jax.numpy as jnp
import numpy as np
from jax import lax
from jax.experimental import pallas as pl
from jax.experimental.pallas import tpu as pltpu

F32 = jnp.float32
BF16 = jnp.bfloat16

HEAD_DIM = 64
LANES = 128
GRID_W = 64
ROPE_THETA = 10000.0
EPS = 1e-6
CONV_W = 3
QUARTER = HEAD_DIM // 4

ROW_TILE = 512
UNIT_COLS = 512
KEY_CHUNK = 256
V_ROWS = LANES + 16
FFN_CHUNKS = 1
HALO = 8
VMEM_LIMIT = 56 * 1024 * 1024


def _cparams():
    return pltpu.CompilerParams(
        dimension_semantics=("parallel", "arbitrary"),
        vmem_limit_bytes=VMEM_LIMIT,
    )


def _rms(x):
    return x * lax.rsqrt(jnp.mean(x * x, axis=-1, keepdims=True) + EPS)


def _lane_tile(a, width):
    return jnp.concatenate([a] * (width // LANES), axis=1)


def _ada_kernel(c_ref, w_ref, b_ref, o_ref):
    c = c_ref[...]
    s = c * jax.nn.sigmoid(c)
    o_ref[...] = jnp.dot(s.astype(BF16), w_ref[...].astype(BF16),
                         preferred_element_type=F32) + b_ref[...]


def _ada_call(c_all, w_ada, b_ada):
    depth, d, d6 = w_ada.shape
    rows = c_all.shape[0]
    tn = d6 // 4
    return pl.pallas_call(
        _ada_kernel,
        out_shape=jax.ShapeDtypeStruct((depth, rows, d6), F32),
        grid=(depth, d6 // tn),
        in_specs=[
            pl.BlockSpec((rows, d), lambda l, j: (0, 0)),
            pl.BlockSpec((None, d, tn), lambda l, j: (l, 0, j)),
            pl.BlockSpec((None, 1, tn), lambda l, j: (l, 0, j)),
        ],
        out_specs=pl.BlockSpec((None, rows, tn), lambda l, j: (l, 0, j)),
        compiler_params=_cparams(),
        name="adaln",
    )(c_all, w_ada, b_ada.reshape(depth, 1, d6))


def _qkv_kernel(*refs, a_heads, a_kv_heads, b_heads, use_rope):
    if use_rope:
        (x_ref, mod_ref, g_ref, wk_ref, wqv_ref, gq_ref, gk_ref, pm_ref,
         cos_ref, sa_ref, sb_ref, cost_ref, sint_ref, qt_ref, k_ref, vt_ref) = refs
    else:
        (x_ref, mod_ref, g_ref, wk_ref, wqv_ref, gq_ref, gk_ref, pm_ref,
         qt_ref, k_ref, vt_ref) = refs
    t = x_ref.shape[0]
    x = x_ref[...]
    h = (_rms(x) * g_ref[...]) * (1.0 + mod_ref[1:2, :]) + mod_ref[0:1, :]
    pk = jnp.dot(h.astype(BF16), wk_ref[...], preferred_element_type=F32)
    pt = jnp.dot(wqv_ref[...], h.T.astype(BF16), preferred_element_type=F32)

    def rope_k(z):
        if not use_rope:
            return z
        return (z * cos_ref[...] + pltpu.roll(z, QUARTER, 1) * sa_ref[...]
                + pltpu.roll(z, LANES - QUARTER, 1) * sb_ref[...])

    ka = pk[:, 0:LANES]
    sq = ka * ka
    hi = sq.astype(BF16)
    lo = (sq - hi.astype(F32)).astype(BF16)
    ms = (jnp.dot(hi, pm_ref[...], preferred_element_type=F32)
          + jnp.dot(lo, pm_ref[...], preferred_element_type=F32))
    k_ref[0] = rope_k(ka * lax.rsqrt(ms + EPS) * gk_ref[...]).astype(BF16)
    for hh in range(b_heads):
        k_ref[1 + hh] = rope_k(pk[:, (1 + hh) * LANES:(2 + hh) * LANES]).astype(BF16)

    def rope_q(z):
        if not use_rope:
            return z
        swap = jnp.concatenate([z[QUARTER:2 * QUARTER], z[0:QUARTER],
                                z[3 * QUARTER:4 * QUARTER], z[2 * QUARTER:3 * QUARTER]], axis=0)
        return z * cost_ref[...] + swap * sint_ref[...]

    q_scale = HEAD_DIM ** -0.5 * math.log2(math.e)
    zero = jnp.zeros((HEAD_DIM, t), BF16)
    gq = _lane_tile(gq_ref[...], t)

    def put_q(blk, half, z):
        qt_ref[blk, half * HEAD_DIM:(half + 1) * HEAD_DIM, :] = (z * q_scale).astype(BF16)
        qt_ref[blk, (1 - half) * HEAD_DIM:(2 - half) * HEAD_DIM, :] = zero

    group = a_heads // a_kv_heads
    for hh in range(a_heads):
        z = pt[hh * HEAD_DIM:(hh + 1) * HEAD_DIM, :]
        z = z * lax.rsqrt(jnp.mean(z * z, axis=0, keepdims=True) + EPS) * gq
        put_q(hh, hh // group, rope_q(z))
    off = a_heads * HEAD_DIM
    ones = jnp.ones((V_ROWS - LANES, t), BF16)
    vt_ref[0, 0:LANES, :] = pt[off:off + LANES, :].astype(BF16)
    vt_ref[0, LANES:V_ROWS, :] = ones
    off += LANES
    for hh in range(b_heads):
        for j in range(2):
            put_q(a_heads + 2 * hh + j, j, rope_q(pt[off:off + HEAD_DIM, :]))
            off += HEAD_DIM
    for hh in range(b_heads):
        vt_ref[1 + hh, 0:LANES, :] = pt[off:off + LANES, :].astype(BF16)
        vt_ref[1 + hh, LANES:V_ROWS, :] = ones
        off += LANES


def _qkv_call(x, mods, g_attn, w_k, w_qvt, gq, gk, pm, rope, *, layer, mod_row, a_heads,
              a_kv_heads, b_heads):
    bsz, s, d = x.shape
    t = min(ROW_TILE, s)
    nt = s // t
    n_kv = 1 + b_heads
    n_qblk = a_heads + 2 * b_heads
    use_rope = rope is not None
    kern = functools.partial(_qkv_kernel, a_heads=a_heads, a_kv_heads=a_kv_heads, b_heads=b_heads,
                             use_rope=use_rope)
    in_specs = [
        pl.BlockSpec((None, t, d), lambda b, i: (b, i, 0)),
        pl.BlockSpec((None, None, 6, d), lambda b, i: (layer, mod_row(b), 0, 0)),
        pl.BlockSpec((None, 1, d), lambda b, i: (layer, 0, 0)),
        pl.BlockSpec((None,) + w_k.shape[1:], lambda b, i: (layer, 0, 0)),
        pl.BlockSpec((None,) + w_qvt.shape[1:], lambda b, i: (layer, 0, 0)),
        pl.BlockSpec((None, HEAD_DIM, LANES), lambda b, i: (layer, 0, 0)),
        pl.BlockSpec((None, 1, LANES), lambda b, i: (layer, 0, 0)),
        pl.BlockSpec((LANES, LANES), lambda b, i: (0, 0)),
    ]
    args = [x, mods, g_attn, w_k, w_qvt, gq, gk, pm]
    if use_rope:
        in_specs += [pl.BlockSpec((t, LANES), lambda b, i: (i, 0))] * 3
        in_specs += [pl.BlockSpec((HEAD_DIM, t), lambda b, i: (0, i))] * 2
        args += list(rope)
    return pl.pallas_call(
        kern,
        out_shape=(
            jax.ShapeDtypeStruct((bsz, nt, n_qblk, LANES, t), BF16),
            jax.ShapeDtypeStruct((bsz, n_kv, s, LANES), BF16),
            jax.ShapeDtypeStruct((bsz, n_kv, V_ROWS, s), BF16),
        ),
        grid=(bsz, nt),
        in_specs=in_specs,
        out_specs=(
            pl.BlockSpec((None, None, n_qblk, LANES, t), lambda b, i: (b, i, 0, 0, 0)),
            pl.BlockSpec((None, n_kv, t, LANES), lambda b, i: (b, 0, i, 0)),
            pl.BlockSpec((None, n_kv, V_ROWS, t), lambda b, i: (b, 0, 0, i)),
        ),
        compiler_params=_cparams(),
        name="qkv_proj_x" if use_rope else "qkv_proj_c",
    )(*args)


def _attn_kernel(*refs, n_seg, a_heads, a_kv_heads, b_heads, lam_init):
    qt_ref = refs[0]
    k_refs = refs[1:1 + n_seg]
    vt_refs = refs[1 + n_seg:1 + 2 * n_seg]
    x_ref, mod_ref, wot_ref, gsub_ref, lam_ref, o_ref, ot_sc, mrg_sc, sa_sc, sb_sc = refs[1 + 2 * n_seg:]
    t = x_ref.shape[0]
    qpu = UNIT_COLS // t
    n_units = qt_ref.shape[0] // qpu

    chunks, row = [], 0
    for seg, k_ref in enumerate(k_refs):
        n = k_ref.shape[1]
        ck = min(n, KEY_CHUNK)
        for c0 in range(0, n, ck):
            chunks.append((seg, c0, ck, row + c0))
        row += n

    def kv_group(u):
        return jnp.maximum(lax.shift_right_arithmetic(u * qpu - a_heads, 1) + 1, 0)

    def block_out(blk):
        return ot_sc.at[blk // qpu, :, (blk % qpu) * t:(blk % qpu + 1) * t]

    def step(u_next, s_next, u_cur, s_cur, m_cur):
        if u_next is not None:
            kv_n = kv_group(u_next)
            qt = jnp.concatenate([qt_ref[qpu * u_next + j] for j in range(qpu)], axis=1)
        if u_cur is not None:
            kv_c = kv_group(u_cur)
        m_next = acc = None
        for seg, c0, ck, r0 in chunks:
            if u_next is not None:
                s = jnp.dot(k_refs[seg][kv_n, c0:c0 + ck, :], qt,
                            preferred_element_type=F32).astype(BF16)
                s_next[r0:r0 + ck, :] = s
                m_c = jnp.max(s, axis=0, keepdims=True)
                m_next = m_c if m_next is None else jnp.maximum(m_next, m_c)
            if u_cur is not None:
                p = jnp.exp2(s_cur[r0:r0 + ck, :] - m_cur)
                part = jnp.dot(vt_refs[seg][kv_c, :, c0:c0 + ck], p, preferred_element_type=F32)
                acc = part if acc is None else acc + part
        if u_cur is not None:
            ot_sc[u_cur] = acc[0:LANES, :] * (1.0 / acc[LANES:LANES + 1, :])
        return m_next

    def unit_pair(i, m_a):
        m_b = step(2 * i + 1, sb_sc, 2 * i, sa_sc, m_a)
        return step(2 * i + 2, sa_sc, 2 * i + 1, sb_sc, m_b)

    assert n_units % 2 == 0
    m_a = step(0, sa_sc, None, None, None)
    m_a = lax.fori_loop(0, n_units // 2 - 1, unit_pair, m_a)
    m_b = step(n_units - 1, sb_sc, n_units - 2, sa_sc, m_a)
    step(None, None, n_units - 1, sb_sc, m_b)

    group = a_heads // a_kv_heads
    for hh in range(a_heads):
        kvh = hh // group
        blk = block_out(hh)[kvh * HEAD_DIM:(kvh + 1) * HEAD_DIM, :]
        mrg_sc[hh * HEAD_DIM:(hh + 1) * HEAD_DIM, :] = blk.astype(BF16)

    lv = lam_ref[...]
    lam = (jnp.exp(jnp.sum(lv[0:1, :] * lv[1:2, :], axis=-1, keepdims=True))
           - jnp.exp(jnp.sum(lv[2:3, :] * lv[3:4, :], axis=-1, keepdims=True)) + lam_init)
    gsub = _lane_tile(gsub_ref[...], t)
    a_w = a_heads * HEAD_DIM
    for hh in range(b_heads):
        diff = block_out(a_heads + 2 * hh)[...] - lam * block_out(a_heads + 2 * hh + 1)[...]
        sub = diff * lax.rsqrt(jnp.mean(diff * diff, axis=0, keepdims=True) + EPS) * gsub
        mrg_sc[a_w + hh * LANES:a_w + (hh + 1) * LANES, :] = (sub * (1.0 - lam_init)).astype(BF16)

    y = lax.dot_general(mrg_sc[...], wot_ref[...], (((0,), (0,)), ((), ())),
                        preferred_element_type=F32)
    o_ref[...] = x_ref[...] + mod_ref[2:3, :] * y


def _attn_call(qt, ks, vts, x, mods, w_ot, gsub, lamv, *, layer, mod_row, a_heads, a_kv_heads,
               b_heads, lam_init):
    bsz, s, d = x.shape
    t = min(ROW_TILE, s)
    nt = s // t
    n_seg = len(ks)
    assert UNIT_COLS % t == 0 and a_heads % (UNIT_COLS // t) == 0
    n_units = qt.shape[2] * t // UNIT_COLS
    n_keys = sum(k.shape[2] for k in ks)
    kern = functools.partial(_attn_kernel, n_seg=n_seg, a_heads=a_heads, a_kv_heads=a_kv_heads,
                             b_heads=b_heads, lam_init=lam_init)
    once = pl.Buffered(1)
    in_specs = [pl.BlockSpec((None, None) + qt.shape[2:], lambda b, i: (b, i, 0, 0, 0))]
    in_specs += [pl.BlockSpec((None,) + kv.shape[1:], lambda b, i: (b, 0, 0, 0)) for kv in ks + vts]
    in_specs += [
        pl.BlockSpec((None, t, d), lambda b, i: (b, i, 0)),
        pl.BlockSpec((None, None, 6, d), lambda b, i: (layer, mod_row(b), 0, 0)),
        pl.BlockSpec((None, d, d), lambda b, i: (layer, 0, 0), pipeline_mode=once),
        pl.BlockSpec((None, LANES, LANES), lambda b, i: (layer, 0, 0)),
        pl.BlockSpec((None, 4, HEAD_DIM), lambda b, i: (layer, 0, 0)),
    ]
    return pl.pallas_call(
        kern,
        out_shape=jax.ShapeDtypeStruct((bsz, s, d), F32),
        grid=(bsz, nt),
        in_specs=in_specs,
        out_specs=pl.BlockSpec((None, t, d), lambda b, i: (b, i, 0)),
        scratch_shapes=[
            pltpu.VMEM((n_units, LANES, UNIT_COLS), F32),
            pltpu.VMEM((d, t), BF16),
        ] + [pltpu.VMEM((n_keys, UNIT_COLS), BF16)] * 2,
        compiler_params=_cparams(),
        name="attention_x" if n_seg == 2 else "attention_c",
    )(qt, *ks, *vts, x, mods, w_ot, gsub, lamv)


def _ffn_kernel(x_ref, xp_ref, xn_ref, mod_ref, g_ref, wup_ref, cw_ref, cb_ref, wdn_ref, gfin_ref,
                o_ref, *, final_norm):
    t = x_ref.shape[0]
    i = pl.program_id(1)

    def norm_mod(x):
        return (_rms(x) * g_ref[...]) * (1.0 + mod_ref[4:5, :]) + mod_ref[3:4, :]

    x = x_ref[...]
    h_prev = jnp.where(i > 0, norm_mod(xp_ref[...]), 0.0)
    h_next = jnp.where(i < pl.num_programs(1) - 1, norm_mod(xn_ref[...]), 0.0)
    h = jnp.concatenate([h_prev, norm_mod(x), h_next], axis=0).astype(BF16)
    rows = t + 2 * HALO

    y = jnp.zeros((t, x.shape[1]), F32)
    for c in range(wup_ref.shape[0]):
        u = jnp.dot(h, wup_ref[c], preferred_element_type=F32)
        cw = cw_ref[c]
        conv = (pltpu.roll(u, 1, 0) * cw[0:1, :] + u * cw[1:2, :]
                + pltpu.roll(u, rows - 1, 0) * cw[2:3, :] + cb_ref[c])
        conv = conv[HALO:HALO + t, :]
        fc = conv.shape[1] // 2
        gate, val = conv[:, :fc], conv[:, fc:]
        hid = (gate * jax.nn.sigmoid(gate)) * val
        y = y + jnp.dot(hid.astype(BF16), wdn_ref[c], preferred_element_type=F32)

    out = x + mod_ref[5:6, :] * y
    if final_norm:
        out = _rms(out) * gfin_ref[...]
    o_ref[...] = out


def _ffn_call(x, mods, g_ffn, w_up, conv_w, conv_b, w_down, g_final, *, layer, mod_row, final_norm):
    bsz, s, d = x.shape
    t = min(ROW_TILE, s)
    nt = s // t
    hpt = t // HALO
    n_halo = s // HALO
    nch, _, fc2 = w_up.shape[1:]
    kern = functools.partial(_ffn_kernel, final_norm=final_norm)
    once = pl.Buffered(1)
    return pl.pallas_call(
        kern,
        out_shape=jax.ShapeDtypeStruct((bsz, s, d), F32),
        grid=(bsz, nt),
        in_specs=[
            pl.BlockSpec((None, t, d), lambda b, i: (b, i, 0)),
            pl.BlockSpec((None, HALO, d), lambda b, i: (b, jnp.maximum(i * hpt - 1, 0), 0)),
            pl.BlockSpec((None, HALO, d),
                         lambda b, i: (b, jnp.minimum((i + 1) * hpt, n_halo - 1), 0)),
            pl.BlockSpec((None, None, 6, d), lambda b, i: (layer, mod_row(b), 0, 0)),
            pl.BlockSpec((None, 1, d), lambda b, i: (layer, 0, 0)),
            pl.BlockSpec((None, nch, d, fc2), lambda b, i: (layer, 0, 0, 0), pipeline_mode=once),
            pl.BlockSpec((None, nch, CONV_W, fc2), lambda b, i: (layer, 0, 0, 0)),
            pl.BlockSpec((None, nch, 1, fc2), lambda b, i: (layer, 0, 0, 0)),
            pl.BlockSpec((None, nch, fc2 // 2, d), lambda b, i: (layer, 0, 0, 0),
                         pipeline_mode=once),
            pl.BlockSpec((1, d), lambda b, i: (0, 0)),
        ],
        out_specs=pl.BlockSpec((None, t, d), lambda b, i: (b, i, 0)),
        compiler_params=_cparams(),
        name="conv_ffn",
    )(x, x, x, mods, g_ffn, w_up, conv_w, conv_b, w_down, g_final)


def _rope_tables(n_lat):
    rows = n_lat // GRID_W
    row = jnp.repeat(jnp.arange(rows, dtype=F32), GRID_W)
    col = jnp.tile(jnp.arange(GRID_W, dtype=F32), rows)
    inv_freq = ROPE_THETA ** (-jnp.arange(QUARTER, dtype=F32) / QUARTER)
    ang_r = row[:, None] * inv_freq[None, :]
    ang_c = col[:, None] * inv_freq[None, :]
    cos = jnp.concatenate([jnp.cos(ang_r), jnp.cos(ang_r), jnp.cos(ang_c), jnp.cos(ang_c)], axis=-1)
    sin = jnp.concatenate([jnp.sin(ang_r), jnp.sin(ang_r), jnp.sin(ang_c), jnp.sin(ang_c)], axis=-1)
    upper = (np.arange(HEAD_DIM) // QUARTER) % 2 == 1
    cos_t = cos.T
    sin_t = jnp.where(upper[None, :], sin, -sin).T
    rep = LANES // HEAD_DIM
    cos2, sin2 = jnp.tile(cos, (1, rep)), jnp.tile(sin, (1, rep))
    upper2 = np.tile(upper, rep)[None, :]
    sin_a = jnp.where(upper2, sin2, 0.0)
    sin_b = jnp.where(upper2, 0.0, -sin2)
    return cos2, sin_a, sin_b, cos_t, sin_t


def kernel(x, c, ctx, c_ctx, w_ada, b_ada, g_attn, w_in, g_qa, g_ka, lam_q1, lam_k1, lam_q2, lam_k2,
           g_sub, w_o, g_ffn, w_up, conv_w, conv_b, w_down, g_final):
    bsz, n_lat, d = x.shape
    ctx_len = ctx.shape[1]
    depth = w_ada.shape[0]
    a_heads, a_kv_heads, b_heads = d // 128, d // 512, d // 256
    a_q, a_kv = a_heads * HEAD_DIM, a_kv_heads * HEAD_DIM
    b_qk, b_v = b_heads * 2 * HEAD_DIM, b_heads * 2 * HEAD_DIM
    f = w_down.shape[1]
    assert a_kv == LANES
    assert all(n % min(ROW_TILE, n) == 0 and UNIT_COLS % min(ROW_TILE, n) == 0 for n in (ctx_len, n_lat))
    assert f % (FFN_CHUNKS * LANES) == 0

    pad = (-(bsz + 1)) % 8
    c_all = jnp.concatenate([c, c_ctx[None, :], jnp.zeros((pad, d), F32)], axis=0)
    mods = _ada_call(c_all, w_ada, b_ada).reshape(depth, bsz + 1 + pad, 6, d)
    row_x = lambda b: b
    row_c = lambda b: bsz

    o_ak, o_av, o_bq, o_bk, o_bv = np.cumsum([a_q, a_kv, a_kv, b_qk, b_qk])
    w_k = jnp.concatenate([w_in[:, :, o_ak:o_av], w_in[:, :, o_bk:o_bv]], axis=-1).astype(BF16)
    w_qvt = jnp.swapaxes(jnp.concatenate(
        [w_in[:, :, :o_ak], w_in[:, :, o_av:o_bq], w_in[:, :, o_bq:o_bk], w_in[:, :, o_bv:]],
        axis=-1), 1, 2).astype(BF16)
    w_ot = w_o.astype(BF16)

    fc = f // FFN_CHUNKS

    def chunk_cols(a):
        g_part = a[..., :f].reshape(a.shape[:-1] + (FFN_CHUNKS, fc))
        v_part = a[..., f:].reshape(a.shape[:-1] + (FFN_CHUNKS, fc))
        return jnp.concatenate([g_part, v_part], axis=-1)

    w_up_c = jnp.moveaxis(chunk_cols(w_up.astype(BF16)), 2, 1)
    conv_w_c = jnp.moveaxis(chunk_cols(conv_w), 2, 1)
    conv_b_c = chunk_cols(conv_b)[:, :, None, :]
    w_down_c = w_down.astype(BF16).reshape(depth, FFN_CHUNKS, fc, d)

    gq = jnp.broadcast_to(g_qa[:, :, None], (depth, HEAD_DIM, LANES))
    gk = jnp.tile(g_ka, (1, LANES // HEAD_DIM))[:, None, :]
    gsub = jnp.broadcast_to(g_sub[:, :, None], (depth, LANES, LANES))
    lamv = jnp.stack([lam_q1, lam_k1, lam_q2, lam_k2], axis=1)
    seg = np.arange(LANES) // HEAD_DIM
    pm = jnp.asarray((seg[:, None] == seg[None, :]) / HEAD_DIM, BF16)
    rope = _rope_tables(n_lat)
    heads = dict(a_heads=a_heads, a_kv_heads=a_kv_heads, b_heads=b_heads)

    xl, xc = x, ctx
    for l in range(depth):
        last = l == depth - 1
        lam_init = 0.8 - 0.6 * math.exp(-0.3 * l)
        qc, kc, vc = _qkv_call(xc, mods, g_attn[:, None, :], w_k, w_qvt, gq, gk, pm, None,
                               layer=l, mod_row=row_c, **heads)
        qx, kx, vx = _qkv_call(xl, mods, g_attn[:, None, :], w_k, w_qvt, gq, gk, pm, rope,
                               layer=l, mod_row=row_x, **heads)
        xl = _attn_call(qx, [kc, kx], [vc, vx], xl, mods, w_ot, gsub, lamv, layer=l, mod_row=row_x,
                        lam_init=lam_init, **heads)
        xl = _ffn_call(xl, mods, g_ffn[:, None, :], w_up_c, conv_w_c, conv_b_c, w_down_c,
                       g_final[None, :], layer=l, mod_row=row_x, final_norm=last)
        if not last:
            xc = _attn_call(qc, [kc], [vc], xc, mods, w_ot, gsub, lamv, layer=l, mod_row=row_c,
                            lam_init=lam_init, **heads)
            xc = _ffn_call(xc, mods, g_ffn[:, None, :], w_up_c, conv_w_c, conv_b_c, w_down_c,
                           g_final[None, :], layer=l, mod_row=row_c, final_norm=False)
    return xl
```

```python
import functools
import math

import jax
import jax.numpy as jnp
import numpy as np
from jax import lax
from jax.experimental import pallas as pl
from jax.experimental.pallas import tpu as pltpu

F32 = jnp.float32
BF16 = jnp.bfloat16

HEAD_DIM = 64
LANES = 128
GRID_W = 64
ROPE_THETA = 10000.0
EPS = 1e-6
CONV_W = 3
QUARTER = HEAD_DIM // 4

ROW_TILE = 512
UNIT_COLS = 512
KEY_CHUNK = 256
V_ROWS = LANES + 16
FFN_CHUNKS = 1
HALO = 8
VMEM_LIMIT = 56 * 1024 * 1024


def _cparams():
    return pltpu.CompilerParams(
        dimension_semantics=("parallel", "arbitrary"),
        vmem_limit_bytes=VMEM_LIMIT,
    )


def _rms(x):
    return x * lax.rsqrt(jnp.mean(x * x, axis=-1, keepdims=True) + EPS)


def _lane_tile(a, width):
    return jnp.concatenate([a] * (width // LANES), axis=1)


def _ada_kernel(c_ref, w_ref, b_ref, o_ref):
    c = c_ref[...]
    s = c * jax.nn.sigmoid(c)
    o_ref[...] = jnp.dot(s.astype(BF16), w_ref[...].astype(BF16),
                         preferred_element_type=F32) + b_ref[...]


def _ada_call(c_all, w_ada, b_ada):
    depth, d, d6 = w_ada.shape
    rows = c_all.shape[0]
    tn = d6 // 4
    return pl.pallas_call(
        _ada_kernel,
        out_shape=jax.ShapeDtypeStruct((depth, rows, d6), F32),
        grid=(depth, d6 // tn),
        in_specs=[
            pl.BlockSpec((rows, d), lambda l, j: (0, 0)),
            pl.BlockSpec((None, d, tn), lambda l, j: (l, 0, j)),
            pl.BlockSpec((None, 1, tn), lambda l, j: (l, 0, j)),
        ],
        out_specs=pl.BlockSpec((None, rows, tn), lambda l, j: (l, 0, j)),
        compiler_params=_cparams(),
        name="adaln",
    )(c_all, w_ada, b_ada.reshape(depth, 1, d6))


def _qkv_kernel(*refs, a_heads, a_kv_heads, b_heads, use_rope):
    if use_rope:
        (x_ref, mod_ref, g_ref, wk_ref, wqv_ref, gq_ref, gk_ref, pm_ref,
         cos_ref, sa_ref, sb_ref, cost_ref, sint_ref, qt_ref, k_ref, vt_ref) = refs
    else:
        (x_ref, mod_ref, g_ref, wk_ref, wqv_ref, gq_ref, gk_ref, pm_ref,
         qt_ref, k_ref, vt_ref) = refs
    t = x_ref.shape[0]
    x = x_ref[...]
    h = (_rms(x) * g_ref[...]) * (1.0 + mod_ref[1:2, :]) + mod_ref[0:1, :]
    pk = jnp.dot(h.astype(BF16), wk_ref[...], preferred_element_type=F32)
    pt = jnp.dot(wqv_ref[...], h.T.astype(BF16), preferred_element_type=F32)

    def rope_k(z):
        if not use_rope:
            return z
        return (z * cos_ref[...] + pltpu.roll(z, QUARTER, 1) * sa_ref[...]
                + pltpu.roll(z, LANES - QUARTER, 1) * sb_ref[...])

    ka = pk[:, 0:LANES]
    sq = ka * ka
    hi = sq.astype(BF16)
    lo = (sq - hi.astype(F32)).astype(BF16)
    ms = (jnp.dot(hi, pm_ref[...], preferred_element_type=F32)
          + jnp.dot(lo, pm_ref[...], preferred_element_type=F32))
    k_ref[0] = rope_k(ka * lax.rsqrt(ms + EPS) * gk_ref[...]).astype(BF16)
    for hh in range(b_heads):
        k_ref[1 + hh] = rope_k(pk[:, (1 + hh) * LANES:(2 + hh) * LANES]).astype(BF16)

    def rope_q(z):
        if not use_rope:
            return z
        swap = jnp.concatenate([z[QUARTER:2 * QUARTER], z[0:QUARTER],
                                z[3 * QUARTER:4 * QUARTER], z[2 * QUARTER:3 * QUARTER]], axis=0)
        return z * cost_ref[...] + swap * sint_ref[...]

    q_scale = HEAD_DIM ** -0.5 * math.log2(math.e)
    zero = jnp.zeros((HEAD_DIM, t), BF16)
    gq = _lane_tile(gq_ref[...], t)

    def put_q(blk, half, z):
        qt_ref[blk, half * HEAD_DIM:(half + 1) * HEAD_DIM, :] = (z * q_scale).astype(BF16)
        qt_ref[blk, (1 - half) * HEAD_DIM:(2 - half) * HEAD_DIM, :] = zero

    group = a_heads // a_kv_heads
    for hh in range(a_heads):
        z = pt[hh * HEAD_DIM:(hh + 1) * HEAD_DIM, :]
        z = z * lax.rsqrt(jnp.mean(z * z, axis=0, keepdims=True) + EPS) * gq
        put_q(hh, hh // group, rope_q(z))
    off = a_heads * HEAD_DIM
    ones = jnp.ones((V_ROWS - LANES, t), BF16)
    vt_ref[0, 0:LANES, :] = pt[off:off + LANES, :].astype(BF16)
    vt_ref[0, LANES:V_ROWS, :] = ones
    off += LANES
    for hh in range(b_heads):
        for j in range(2):
            put_q(a_heads + 2 * hh + j, j, rope_q(pt[off:off + HEAD_DIM, :]))
            off += HEAD_DIM
    for hh in range(b_heads):
        vt_ref[1 + hh, 0:LANES, :] = pt[off:off + LANES, :].astype(BF16)
        vt_ref[1 + hh, LANES:V_ROWS, :] = ones
        off += LANES


def _qkv_call(x, mods, g_attn, w_k, w_qvt, gq, gk, pm, rope, *, layer, mod_row, a_heads,
              a_kv_heads, b_heads):
    bsz, s, d = x.shape
    t = min(ROW_TILE, s)
    nt = s // t
    n_kv = 1 + b_heads
    n_qblk = a_heads + 2 * b_heads
    use_rope = rope is not None
    kern = functools.partial(_qkv_kernel, a_heads=a_heads, a_kv_heads=a_kv_heads, b_heads=b_heads,
                             use_rope=use_rope)
    in_specs = [
        pl.BlockSpec((None, t, d), lambda b, i: (b, i, 0)),
        pl.BlockSpec((None, None, 6, d), lambda b, i: (layer, mod_row(b), 0, 0)),
        pl.BlockSpec((None, 1, d), lambda b, i: (layer, 0, 0)),
        pl.BlockSpec((None,) + w_k.shape[1:], lambda b, i: (layer, 0, 0)),
        pl.BlockSpec((None,) + w_qvt.shape[1:], lambda b, i: (layer, 0, 0)),
        pl.BlockSpec((None, HEAD_DIM, LANES), lambda b, i: (layer, 0, 0)),
        pl.BlockSpec((None, 1, LANES), lambda b, i: (layer, 0, 0)),
        pl.BlockSpec((LANES, LANES), lambda b, i: (0, 0)),
    ]
    args = [x, mods, g_attn, w_k, w_qvt, gq, gk, pm]
    if use_rope:
        in_specs += [pl.BlockSpec((t, LANES), lambda b, i: (i, 0))] * 3
        in_specs += [pl.BlockSpec((HEAD_DIM, t), lambda b, i: (0, i))] * 2
        args += list(rope)
    return pl.pallas_call(
        kern,
        out_shape=(
            jax.ShapeDtypeStruct((bsz, nt, n_qblk, LANES, t), BF16),
            jax.ShapeDtypeStruct((bsz, n_kv, s, LANES), BF16),
            jax.ShapeDtypeStruct((bsz, n_kv, V_ROWS, s), BF16),
        ),
        grid=(bsz, nt),
        in_specs=in_specs,
        out_specs=(
            pl.BlockSpec((None, None, n_qblk, LANES, t), lambda b, i: (b, i, 0, 0, 0)),
            pl.BlockSpec((None, n_kv, t, LANES), lambda b, i: (b, 0, i, 0)),
            pl.BlockSpec((None, n_kv, V_ROWS, t), lambda b, i: (b, 0, 0, i)),
        ),
        compiler_params=_cparams(),
        name="qkv_proj_x" if use_rope else "qkv_proj_c",
    )(*args)


def _attn_kernel(*refs, n_seg, a_heads, a_kv_heads, b_heads, lam_init):
    qt_ref = refs[0]
    k_refs = refs[1:1 + n_seg]
    vt_refs = refs[1 + n_seg:1 + 2 * n_seg]
    x_ref, mod_ref, wot_ref, gsub_ref, lam_ref, o_ref, ot_sc, mrg_sc, sa_sc, sb_sc = refs[1 + 2 * n_seg:]
    t = x_ref.shape[0]
    qpu = UNIT_COLS // t
    n_units = qt_ref.shape[0] // qpu

    chunks, row = [], 0
    for seg, k_ref in enumerate(k_refs):
        n = k_ref.shape[1]
        ck = min(n, KEY_CHUNK)
        for c0 in range(0, n, ck):
            chunks.append((seg, c0, ck, row + c0))
        row += n

    def kv_group(u):
        return jnp.maximum(lax.shift_right_arithmetic(u * qpu - a_heads, 1) + 1, 0)

    def block_out(blk):
        return ot_sc.at[blk // qpu, :, (blk % qpu) * t:(blk % qpu + 1) * t]

    def step(u_next, s_next, u_cur, s_cur, m_cur):
        if u_next is not None:
            kv_n = kv_group(u_next)
            qt = jnp.concatenate([qt_ref[qpu * u_next + j] for j in range(qpu)], axis=1)
        if u_cur is not None:
            kv_c = kv_group(u_cur)
        m_next = acc = shift = None
        for seg, c0, ck, r0 in chunks:
            if u_next is not None:
                s = jnp.dot(k_refs[seg][kv_n, c0:c0 + ck, :], qt, preferred_element_type=F32)
                if shift is None:
                    shift = jnp.max(s, axis=0, keepdims=True)
                s = (s - shift).astype(BF16)
                s_next[r0:r0 + ck, :] = s
                m_c = jnp.max(s, axis=0, keepdims=True)
                m_next = m_c if m_next is None else jnp.maximum(m_next, m_c)
            if u_cur is not None:
                p = jnp.exp2(s_cur[r0:r0 + ck, :] - m_cur)
                part = jnp.dot(vt_refs[seg][kv_c, :, c0:c0 + ck], p, preferred_element_type=F32)
                acc = part if acc is None else acc + part
        if u_cur is not None:
            ot_sc[u_cur] = acc[0:LANES, :] * (1.0 / acc[LANES:LANES + 1, :])
        return m_next

    def unit_pair(i, m_a):
        m_b = step(2 * i + 1, sb_sc, 2 * i, sa_sc, m_a)
        return step(2 * i + 2, sa_sc, 2 * i + 1, sb_sc, m_b)

    assert n_units % 2 == 0
    m_a = step(0, sa_sc, None, None, None)
    m_a = lax.fori_loop(0, n_units // 2 - 1, unit_pair, m_a)
    m_b = step(n_units - 1, sb_sc, n_units - 2, sa_sc, m_a)
    step(None, None, n_units - 1, sb_sc, m_b)

    group = a_heads // a_kv_heads
    for hh in range(a_heads):
        kvh = hh // group
        blk = block_out(hh)[kvh * HEAD_DIM:(kvh + 1) * HEAD_DIM, :]
        mrg_sc[hh * HEAD_DIM:(hh + 1) * HEAD_DIM, :] = blk.astype(BF16)

    lv = lam_ref[...]
    lam = (jnp.exp(jnp.sum(lv[0:1, :] * lv[1:2, :], axis=-1, keepdims=True))
           - jnp.exp(jnp.sum(lv[2:3, :] * lv[3:4, :], axis=-1, keepdims=True)) + lam_init)
    gsub = _lane_tile(gsub_ref[...], t)
    a_w = a_heads * HEAD_DIM
    for hh in range(b_heads):
        diff = block_out(a_heads + 2 * hh)[...] - lam * block_out(a_heads + 2 * hh + 1)[...]
        sub = diff * lax.rsqrt(jnp.mean(diff * diff, axis=0, keepdims=True) + EPS) * gsub
        mrg_sc[a_w + hh * LANES:a_w + (hh + 1) * LANES, :] = (sub * (1.0 - lam_init)).astype(BF16)

    y = lax.dot_general(mrg_sc[...], wot_ref[...], (((0,), (0,)), ((), ())),
                        preferred_element_type=F32)
    o_ref[...] = x_ref[...] + mod_ref[2:3, :] * y


def _attn_call(qt, ks, vts, x, mods, w_ot, gsub, lamv, *, layer, mod_row, a_heads, a_kv_heads,
               b_heads, lam_init):
    bsz, s, d = x.shape
    t = min(ROW_TILE, s)
    nt = s // t
    n_seg = len(ks)
    assert UNIT_COLS % t == 0 and a_heads % (UNIT_COLS // t) == 0
    n_units = qt.shape[2] * t // UNIT_COLS
    n_keys = sum(k.shape[2] for k in ks)
    kern = functools.partial(_attn_kernel, n_seg=n_seg, a_heads=a_heads, a_kv_heads=a_kv_heads,
                             b_heads=b_heads, lam_init=lam_init)
    once = pl.Buffered(1)
    in_specs = [pl.BlockSpec((None, None) + qt.shape[2:], lambda b, i: (b, i, 0, 0, 0))]
    in_specs += [pl.BlockSpec((None,) + kv.shape[1:], lambda b, i: (b, 0, 0, 0)) for kv in ks + vts]
    in_specs += [
        pl.BlockSpec((None, t, d), lambda b, i: (b, i, 0)),
        pl.BlockSpec((None, None, 6, d), lambda b, i: (layer, mod_row(b), 0, 0)),
        pl.BlockSpec((None, d, d), lambda b, i: (layer, 0, 0), pipeline_mode=once),
        pl.BlockSpec((None, LANES, LANES), lambda b, i: (layer, 0, 0)),
        pl.BlockSpec((None, 4, HEAD_DIM), lambda b, i: (layer, 0, 0)),
    ]
    return pl.pallas_call(
        kern,
        out_shape=jax.ShapeDtypeStruct((bsz, s, d), F32),
        grid=(bsz, nt),
        in_specs=in_specs,
        out_specs=pl.BlockSpec((None, t, d), lambda b, i: (b, i, 0)),
        scratch_shapes=[
            pltpu.VMEM((n_units, LANES, UNIT_COLS), F32),
            pltpu.VMEM((d, t), BF16),
        ] + [pltpu.VMEM((n_keys, UNIT_COLS), BF16)] * 2,
        compiler_params=_cparams(),
        name="attention_x" if n_seg == 2 else "attention_c",
    )(qt, *ks, *vts, x, mods, w_ot, gsub, lamv)


def _ffn_kernel(x_ref, xp_ref, xn_ref, mod_ref, g_ref, wup_ref, cw_ref, cb_ref, wdn_ref, gfin_ref,
                o_ref, *, final_norm):
    t = x_ref.shape[0]
    i = pl.program_id(1)

    def norm_mod(x):
        return (_rms(x) * g_ref[...]) * (1.0 + mod_ref[4:5, :]) + mod_ref[3:4, :]

    x = x_ref[...]
    h_prev = jnp.where(i > 0, norm_mod(xp_ref[...]), 0.0)
    h_next = jnp.where(i < pl.num_programs(1) - 1, norm_mod(xn_ref[...]), 0.0)
    h = jnp.concatenate([h_prev, norm_mod(x), h_next], axis=0).astype(BF16)
    rows = t + 2 * HALO

    y = jnp.zeros((t, x.shape[1]), F32)
    for c in range(wup_ref.shape[0]):
        u = jnp.dot(h, wup_ref[c], preferred_element_type=F32)
        cw = cw_ref[c]
        conv = (pltpu.roll(u, 1, 0) * cw[0:1, :] + u * cw[1:2, :]
                + pltpu.roll(u, rows - 1, 0) * cw[2:3, :] + cb_ref[c])
        conv = conv[HALO:HALO + t, :]
        fc = conv.shape[1] // 2
        gate, val = conv[:, :fc], conv[:, fc:]
        hid = (gate * jax.nn.sigmoid(gate)) * val
        y = y + jnp.dot(hid.astype(BF16), wdn_ref[c], preferred_element_type=F32)

    out = x + mod_ref[5:6, :] * y
    if final_norm:
        out = _rms(out) * gfin_ref[...]
    o_ref[...] = out


def _ffn_call(x, mods, g_ffn, w_up, conv_w, conv_b, w_down, g_final, *, layer, mod_row, final_norm):
    bsz, s, d = x.shape
    t = min(ROW_TILE, s)
    nt = s // t
    hpt = t // HALO
    n_halo = s // HALO
    nch, _, fc2 = w_up.shape[1:]
    kern = functools.partial(_ffn_kernel, final_norm=final_norm)
    once = pl.Buffered(1)
    return pl.pallas_call(
        kern,
        out_shape=jax.ShapeDtypeStruct((bsz, s, d), F32),
        grid=(bsz, nt),
        in_specs=[
            pl.BlockSpec((None, t, d), lambda b, i: (b, i, 0)),
            pl.BlockSpec((None, HALO, d), lambda b, i: (b, jnp.maximum(i * hpt - 1, 0), 0)),
            pl.BlockSpec((None, HALO, d),
                         lambda b, i: (b, jnp.minimum((i + 1) * hpt, n_halo - 1), 0)),
            pl.BlockSpec((None, None, 6, d), lambda b, i: (layer, mod_row(b), 0, 0)),
            pl.BlockSpec((None, 1, d), lambda b, i: (layer, 0, 0)),
            pl.BlockSpec((None, nch, d, fc2), lambda b, i: (layer, 0, 0, 0), pipeline_mode=once),
            pl.BlockSpec((None, nch, CONV_W, fc2), lambda b, i: (layer, 0, 0, 0)),
            pl.BlockSpec((None, nch, 1, fc2), lambda b, i: (layer, 0, 0, 0)),
            pl.BlockSpec((None, nch, fc2 // 2, d), lambda b, i: (layer, 0, 0, 0),
                         pipeline_mode=once),
            pl.BlockSpec((1, d), lambda b, i: (0, 0)),
        ],
        out_specs=pl.BlockSpec((None, t, d), lambda b, i: (b, i, 0)),
        compiler_params=_cparams(),
        name="conv_ffn",
    )(x, x, x, mods, g_ffn, w_up, conv_w, conv_b, w_down, g_final)


def _rope_tables(n_lat):
    rows = n_lat // GRID_W
    row = jnp.repeat(jnp.arange(rows, dtype=F32), GRID_W)
    col = jnp.tile(jnp.arange(GRID_W, dtype=F32), rows)
    inv_freq = ROPE_THETA ** (-jnp.arange(QUARTER, dtype=F32) / QUARTER)
    ang_r = row[:, None] * inv_freq[None, :]
    ang_c = col[:, None] * inv_freq[None, :]
    cos = jnp.concatenate([jnp.cos(ang_r), jnp.cos(ang_r), jnp.cos(ang_c), jnp.cos(ang_c)], axis=-1)
    sin = jnp.concatenate([jnp.sin(ang_r), jnp.sin(ang_r), jnp.sin(ang_c), jnp.sin(ang_c)], axis=-1)
    upper = (np.arange(HEAD_DIM) // QUARTER) % 2 == 1
    cos_t = cos.T
    sin_t = jnp.where(upper[None, :], sin, -sin).T
    rep = LANES // HEAD_DIM
    cos2, sin2 = jnp.tile(cos, (1, rep)), jnp.tile(sin, (1, rep))
    upper2 = np.tile(upper, rep)[None, :]
    sin_a = jnp.where(upper2, sin2, 0.0)
    sin_b = jnp.where(upper2, 0.0, -sin2)
    return cos2, sin_a, sin_b, cos_t, sin_t


def kernel(x, c, ctx, c_ctx, w_ada, b_ada, g_attn, w_in, g_qa, g_ka, lam_q1, lam_k1, lam_q2, lam_k2,
           g_sub, w_o, g_ffn, w_up, conv_w, conv_b, w_down, g_final):
    bsz, n_lat, d = x.shape
    ctx_len = ctx.shape[1]
    depth = w_ada.shape[0]
    a_heads, a_kv_heads, b_heads = d // 128, d // 512, d // 256
    a_q, a_kv = a_heads * HEAD_DIM, a_kv_heads * HEAD_DIM
    b_qk, b_v = b_heads * 2 * HEAD_DIM, b_heads * 2 * HEAD_DIM
    f = w_down.shape[1]
    assert a_kv == LANES
    assert all(n % min(ROW_TILE, n) == 0 and UNIT_COLS % min(ROW_TILE, n) == 0 for n in (ctx_len, n_lat))
    assert f % (FFN_CHUNKS * LANES) == 0

    pad = (-(bsz + 1)) % 8
    c_all = jnp.concatenate([c, c_ctx[None, :], jnp.zeros((pad, d), F32)], axis=0)
    mods = _ada_call(c_all, w_ada, b_ada).reshape(depth, bsz + 1 + pad, 6, d)
    row_x = lambda b: b
    row_c = lambda b: bsz

    o_ak, o_av, o_bq, o_bk, o_bv = np.cumsum([a_q, a_kv, a_kv, b_qk, b_qk])
    w_k = jnp.concatenate([w_in[:, :, o_ak:o_av], w_in[:, :, o_bk:o_bv]], axis=-1).astype(BF16)
    w_qvt = jnp.swapaxes(jnp.concatenate(
        [w_in[:, :, :o_ak], w_in[:, :, o_av:o_bq], w_in[:, :, o_bq:o_bk], w_in[:, :, o_bv:]],
        axis=-1), 1, 2).astype(BF16)
    w_ot = w_o.astype(BF16)

    fc = f // FFN_CHUNKS

    def chunk_cols(a):
        g_part = a[..., :f].reshape(a.shape[:-1] + (FFN_CHUNKS, fc))
        v_part = a[..., f:].reshape(a.shape[:-1] + (FFN_CHUNKS, fc))
        return jnp.concatenate([g_part, v_part], axis=-1)

    w_up_c = jnp.moveaxis(chunk_cols(w_up.astype(BF16)), 2, 1)
    conv_w_c = jnp.moveaxis(chunk_cols(conv_w), 2, 1)
    conv_b_c = chunk_cols(conv_b)[:, :, None, :]
    w_down_c = w_down.astype(BF16).reshape(depth, FFN_CHUNKS, fc, d)

    gq = jnp.broadcast_to(g_qa[:, :, None], (depth, HEAD_DIM, LANES))
    gk = jnp.tile(g_ka, (1, LANES // HEAD_DIM))[:, None, :]
    gsub = jnp.broadcast_to(g_sub[:, :, None], (depth, LANES, LANES))
    lamv = jnp.stack([lam_q1, lam_k1, lam_q2, lam_k2], axis=1)
    seg = np.arange(LANES) // HEAD_DIM
    pm = jnp.asarray((seg[:, None] == seg[None, :]) / HEAD_DIM, BF16)
    rope = _rope_tables(n_lat)
    heads = dict(a_heads=a_heads, a_kv_heads=a_kv_heads, b_heads=b_heads)

    xl, xc = x, ctx
    for l in range(depth):
        last = l == depth - 1
        lam_init = 0.8 - 0.6 * math.exp(-0.3 * l)
        qc, kc, vc = _qkv_call(xc, mods, g_attn[:, None, :], w_k, w_qvt, gq, gk, pm, None,
                               layer=l, mod_row=row_c, **heads)
        qx, kx, vx = _qkv_call(xl, mods, g_attn[:, None, :], w_k, w_qvt, gq, gk, pm, rope,
                               layer=l, mod_row=row_x, **heads)
        xl = _attn_call(qx, [kc, kx], [vc, vx], xl, mods, w_ot, gsub, lamv, layer=l, mod_row=row_x,
                        lam_init=lam_init, **heads)
        xl = _ffn_call(xl, mods, g_ffn[:, None, :], w_up_c, conv_w_c, conv_b_c, w_down_c,
                       g_final[None, :], layer=l, mod_row=row_x, final_norm=last)
        if not last:
            xc = _attn_call(qc, [kc], [vc], xc, mods, w_ot, gsub, lamv, layer=l, mod_row=row_c,
                            lam_init=lam_init, **heads)
            xc = _ffn_call(xc, mods, g_ffn[:, None, :], w_up_c, conv_w_c, conv_b_c, w_down_c,
                           g_final[None, :], layer=l, mod_row=row_c, final_norm=False)
    return xl
```

```python
import functools
import math

import jax
import jax.numpy as jnp
import numpy as np
from jax import lax
from jax.experimental import pallas as pl
from jax.experimental.pallas import tpu as pltpu

F32 = jnp.float32
BF16 = jnp.bfloat16

HEAD_DIM = 64
LANES = 128
GRID_W = 64
ROPE_THETA = 10000.0
EPS = 1e-6
CONV_W = 3
QUARTER = HEAD_DIM // 4

ROW_TILE = 512
UNIT_COLS = 512
KEY_CHUNK = 256
V_ROWS = LANES + 16
FFN_CHUNKS = 1
ROW_SPLIT = 2
HALO = 8
VMEM_LIMIT = 56 * 1024 * 1024


def _cparams():
    return pltpu.CompilerParams(
        dimension_semantics=("parallel", "arbitrary"),
        vmem_limit_bytes=VMEM_LIMIT,
    )


def _rms(x):
    return x * lax.rsqrt(jnp.mean(x * x, axis=-1, keepdims=True) + EPS)


def _lane_tile(a, width):
    return jnp.concatenate([a] * (width // LANES), axis=1)


def _ada_kernel(c_ref, w_ref, b_ref, o_ref):
    c = c_ref[...]
    s = c * jax.nn.sigmoid(c)
    o_ref[...] = jnp.dot(s.astype(BF16), w_ref[...].astype(BF16),
                         preferred_element_type=F32) + b_ref[...]


def _ada_call(c_all, w_ada, b_ada):
    depth, d, d6 = w_ada.shape
    rows = c_all.shape[0]
    tn = d6 // 4
    return pl.pallas_call(
        _ada_kernel,
        out_shape=jax.ShapeDtypeStruct((depth, rows, d6), F32),
        grid=(depth, d6 // tn),
        in_specs=[
            pl.BlockSpec((rows, d), lambda l, j: (0, 0)),
            pl.BlockSpec((None, d, tn), lambda l, j: (l, 0, j)),
            pl.BlockSpec((None, 1, tn), lambda l, j: (l, 0, j)),
        ],
        out_specs=pl.BlockSpec((None, rows, tn), lambda l, j: (l, 0, j)),
        compiler_params=_cparams(),
        name="adaln",
    )(c_all, w_ada, b_ada.reshape(depth, 1, d6))


def _qkv_kernel(*refs, a_heads, a_kv_heads, b_heads, use_rope):
    if use_rope:
        (x_ref, mod_ref, g_ref, wk_ref, wqv_ref, gq_ref, gk_ref, pm_ref,
         cos_ref, sa_ref, sb_ref, cost_ref, sint_ref, qt_ref, k_ref, vt_ref) = refs
    else:
        (x_ref, mod_ref, g_ref, wk_ref, wqv_ref, gq_ref, gk_ref, pm_ref,
         qt_ref, k_ref, vt_ref) = refs
    t = x_ref.shape[0]
    x = x_ref[...]
    h = (_rms(x) * g_ref[...]) * (1.0 + mod_ref[1:2, :]) + mod_ref[0:1, :]
    pk = jnp.dot(h.astype(BF16), wk_ref[...], preferred_element_type=F32)
    pt = jnp.dot(wqv_ref[...], h.T.astype(BF16), preferred_element_type=F32)

    def rope_k(z):
        if not use_rope:
            return z
        return (z * cos_ref[...] + pltpu.roll(z, QUARTER, 1) * sa_ref[...]
                + pltpu.roll(z, LANES - QUARTER, 1) * sb_ref[...])

    ka = pk[:, 0:LANES]
    sq = ka * ka
    hi = sq.astype(BF16)
    lo = (sq - hi.astype(F32)).astype(BF16)
    ms = (jnp.dot(hi, pm_ref[...], preferred_element_type=F32)
          + jnp.dot(lo, pm_ref[...], preferred_element_type=F32))
    k_ref[0] = rope_k(ka * lax.rsqrt(ms + EPS) * gk_ref[...]).astype(BF16)
    for hh in range(b_heads):
        k_ref[1 + hh] = rope_k(pk[:, (1 + hh) * LANES:(2 + hh) * LANES]).astype(BF16)

    def rope_q(z):
        if not use_rope:
            return z
        swap = jnp.concatenate([z[QUARTER:2 * QUARTER], z[0:QUARTER],
                                z[3 * QUARTER:4 * QUARTER], z[2 * QUARTER:3 * QUARTER]], axis=0)
        return z * cost_ref[...] + swap * sint_ref[...]

    q_scale = HEAD_DIM ** -0.5 * math.log2(math.e)
    zero = jnp.zeros((HEAD_DIM, t), BF16)
    gq = _lane_tile(gq_ref[...], t)

    def put_q(blk, half, z):
        qt_ref[blk, half * HEAD_DIM:(half + 1) * HEAD_DIM, :] = (z * q_scale).astype(BF16)
        qt_ref[blk, (1 - half) * HEAD_DIM:(2 - half) * HEAD_DIM, :] = zero

    group = a_heads // a_kv_heads
    for hh in range(a_heads):
        z = pt[hh * HEAD_DIM:(hh + 1) * HEAD_DIM, :]
        z = z * lax.rsqrt(jnp.mean(z * z, axis=0, keepdims=True) + EPS) * gq
        put_q(hh, hh // group, rope_q(z))
    off = a_heads * HEAD_DIM
    ones = jnp.ones((V_ROWS - LANES, t), BF16)
    vt_ref[0, 0:LANES, :] = pt[off:off + LANES, :].astype(BF16)
    vt_ref[0, LANES:V_ROWS, :] = ones
    off += LANES
    for hh in range(b_heads):
        for j in range(2):
            put_q(a_heads + 2 * hh + j, j, rope_q(pt[off:off + HEAD_DIM, :]))
            off += HEAD_DIM
    for hh in range(b_heads):
        vt_ref[1 + hh, 0:LANES, :] = pt[off:off + LANES, :].astype(BF16)
        vt_ref[1 + hh, LANES:V_ROWS, :] = ones
        off += LANES


def _qkv_call(x, mods, g_attn, w_k, w_qvt, gq, gk, pm, rope, *, layer, mod_row, a_heads,
              a_kv_heads, b_heads):
    bsz, s, d = x.shape
    t = min(ROW_TILE, s)
    nt = s // t
    n_kv = 1 + b_heads
    n_qblk = a_heads + 2 * b_heads
    use_rope = rope is not None
    kern = functools.partial(_qkv_kernel, a_heads=a_heads, a_kv_heads=a_kv_heads, b_heads=b_heads,
                             use_rope=use_rope)
    in_specs = [
        pl.BlockSpec((None, t, d), lambda b, i: (b, i, 0)),
        pl.BlockSpec((None, None, 6, d), lambda b, i: (layer, mod_row(b), 0, 0)),
        pl.BlockSpec((None, 1, d), lambda b, i: (layer, 0, 0)),
        pl.BlockSpec((None,) + w_k.shape[1:], lambda b, i: (layer, 0, 0)),
        pl.BlockSpec((None,) + w_qvt.shape[1:], lambda b, i: (layer, 0, 0)),
        pl.BlockSpec((None, HEAD_DIM, LANES), lambda b, i: (layer, 0, 0)),
        pl.BlockSpec((None, 1, LANES), lambda b, i: (layer, 0, 0)),
        pl.BlockSpec((LANES, LANES), lambda b, i: (0, 0)),
    ]
    args = [x, mods, g_attn, w_k, w_qvt, gq, gk, pm]
    if use_rope:
        in_specs += [pl.BlockSpec((t, LANES), lambda b, i: (i, 0))] * 3
        in_specs += [pl.BlockSpec((HEAD_DIM, t), lambda b, i: (0, i))] * 2
        args += list(rope)
    return pl.pallas_call(
        kern,
        out_shape=(
            jax.ShapeDtypeStruct((bsz, nt, n_qblk, LANES, t), BF16),
            jax.ShapeDtypeStruct((bsz, n_kv, s, LANES), BF16),
            jax.ShapeDtypeStruct((bsz, n_kv, V_ROWS, s), BF16),
        ),
        grid=(bsz, nt),
        in_specs=in_specs,
        out_specs=(
            pl.BlockSpec((None, None, n_qblk, LANES, t), lambda b, i: (b, i, 0, 0, 0)),
            pl.BlockSpec((None, n_kv, t, LANES), lambda b, i: (b, 0, i, 0)),
            pl.BlockSpec((None, n_kv, V_ROWS, t), lambda b, i: (b, 0, 0, i)),
        ),
        compiler_params=_cparams(),
        name="qkv_proj_x" if use_rope else "qkv_proj_c",
    )(*args)


def _attn_kernel(*refs, n_seg, a_heads, a_kv_heads, b_heads, lam_init):
    qt_ref = refs[0]
    k_refs = refs[1:1 + n_seg]
    vt_refs = refs[1 + n_seg:1 + 2 * n_seg]
    x_ref, mod_ref, wot_ref, gsub_ref, lam_ref, o_ref, ot_sc, mrg_sc, sa_sc, sb_sc = refs[1 + 2 * n_seg:]
    t = x_ref.shape[0]
    qpu = UNIT_COLS // t
    n_units = qt_ref.shape[0] // qpu

    chunks, row = [], 0
    for seg, k_ref in enumerate(k_refs):
        n = k_ref.shape[1]
        ck = min(n, KEY_CHUNK)
        for c0 in range(0, n, ck):
            chunks.append((seg, c0, ck, row + c0))
        row += n

    def kv_group(u):
        return jnp.maximum(lax.shift_right_arithmetic(u * qpu - a_heads, 1) + 1, 0)

    def block_out(blk):
        return ot_sc.at[blk // qpu, :, (blk % qpu) * t:(blk % qpu + 1) * t]

    def step(u_next, s_next, u_cur, s_cur, m_cur):
        if u_next is not None:
            kv_n = kv_group(u_next)
            qt = jnp.concatenate([qt_ref[qpu * u_next + j] for j in range(qpu)], axis=1)
        if u_cur is not None:
            kv_c = kv_group(u_cur)
        m_next = acc = None
        for seg, c0, ck, r0 in chunks:
            if u_next is not None:
                s = jnp.dot(k_refs[seg][kv_n, c0:c0 + ck, :], qt,
                            preferred_element_type=F32).astype(BF16)
                s_next[r0:r0 + ck, :] = s
                m_c = jnp.max(s, axis=0, keepdims=True)
                m_next = m_c if m_next is None else jnp.maximum(m_next, m_c)
            if u_cur is not None:
                p = jnp.exp2(s_cur[r0:r0 + ck, :] - m_cur)
                part = jnp.dot(vt_refs[seg][kv_c, :, c0:c0 + ck], p, preferred_element_type=F32)
                acc = part if acc is None else acc + part
        if u_cur is not None:
            ot_sc[u_cur] = acc[0:LANES, :] * (1.0 / acc[LANES:LANES + 1, :])
        return m_next

    def unit_pair(i, m_a):
        m_b = step(2 * i + 1, sb_sc, 2 * i, sa_sc, m_a)
        return step(2 * i + 2, sa_sc, 2 * i + 1, sb_sc, m_b)

    assert n_units % 2 == 0
    m_a = step(0, sa_sc, None, None, None)
    m_a = lax.fori_loop(0, n_units // 2 - 1, unit_pair, m_a)
    m_b = step(n_units - 1, sb_sc, n_units - 2, sa_sc, m_a)
    step(None, None, n_units - 1, sb_sc, m_b)

    group = a_heads // a_kv_heads
    for hh in range(a_heads):
        kvh = hh // group
        blk = block_out(hh)[kvh * HEAD_DIM:(kvh + 1) * HEAD_DIM, :]
        mrg_sc[hh * HEAD_DIM:(hh + 1) * HEAD_DIM, :] = blk.astype(BF16)

    lv = lam_ref[...]
    lam = (jnp.exp(jnp.sum(lv[0:1, :] * lv[1:2, :], axis=-1, keepdims=True))
           - jnp.exp(jnp.sum(lv[2:3, :] * lv[3:4, :], axis=-1, keepdims=True)) + lam_init)
    gsub = _lane_tile(gsub_ref[...], t)
    a_w = a_heads * HEAD_DIM
    for hh in range(b_heads):
        diff = block_out(a_heads + 2 * hh)[...] - lam * block_out(a_heads + 2 * hh + 1)[...]
        sub = diff * lax.rsqrt(jnp.mean(diff * diff, axis=0, keepdims=True) + EPS) * gsub
        mrg_sc[a_w + hh * LANES:a_w + (hh + 1) * LANES, :] = (sub * (1.0 - lam_init)).astype(BF16)

    y = lax.dot_general(mrg_sc[...], wot_ref[...], (((0,), (0,)), ((), ())),
                        preferred_element_type=F32)
    o_ref[...] = x_ref[...] + mod_ref[2:3, :] * y


def _attn_call(qt, ks, vts, x, mods, w_ot, gsub, lamv, *, layer, mod_row, a_heads, a_kv_heads,
               b_heads, lam_init):
    bsz, s, d = x.shape
    t = min(ROW_TILE, s)
    nt = s // t
    n_seg = len(ks)
    assert UNIT_COLS % t == 0 and a_heads % (UNIT_COLS // t) == 0
    n_units = qt.shape[2] * t // UNIT_COLS
    n_keys = sum(k.shape[2] for k in ks)
    kern = functools.partial(_attn_kernel, n_seg=n_seg, a_heads=a_heads, a_kv_heads=a_kv_heads,
                             b_heads=b_heads, lam_init=lam_init)
    once = pl.Buffered(1)
    in_specs = [pl.BlockSpec((None, None) + qt.shape[2:], lambda b, i: (b, i, 0, 0, 0))]
    in_specs += [pl.BlockSpec((None,) + kv.shape[1:], lambda b, i: (b, 0, 0, 0)) for kv in ks + vts]
    in_specs += [
        pl.BlockSpec((None, t, d), lambda b, i: (b, i, 0)),
        pl.BlockSpec((None, None, 6, d), lambda b, i: (layer, mod_row(b), 0, 0)),
        pl.BlockSpec((None, d, d), lambda b, i: (layer, 0, 0), pipeline_mode=once),
        pl.BlockSpec((None, LANES, LANES), lambda b, i: (layer, 0, 0)),
        pl.BlockSpec((None, 4, HEAD_DIM), lambda b, i: (layer, 0, 0)),
    ]
    return pl.pallas_call(
        kern,
        out_shape=jax.ShapeDtypeStruct((bsz, s, d), F32),
        grid=(bsz, nt),
        in_specs=in_specs,
        out_specs=pl.BlockSpec((None, t, d), lambda b, i: (b, i, 0)),
        scratch_shapes=[
            pltpu.VMEM((n_units, LANES, UNIT_COLS), F32),
            pltpu.VMEM((d, t), BF16),
        ] + [pltpu.VMEM((n_keys, UNIT_COLS), BF16)] * 2,
        compiler_params=_cparams(),
        name="attention_x" if n_seg == 2 else "attention_c",
    )(qt, *ks, *vts, x, mods, w_ot, gsub, lamv)


def _ffn_kernel(x_ref, xp_ref, xn_ref, mod_ref, g_ref, wup_ref, cw_ref, cb_ref, wdn_ref, gfin_ref,
                o_ref, *, final_norm):
    t = x_ref.shape[0]
    i = pl.program_id(1)

    def norm_mod(x):
        return (_rms(x) * g_ref[...]) * (1.0 + mod_ref[4:5, :]) + mod_ref[3:4, :]

    x = x_ref[...]
    h_prev = jnp.where(i > 0, norm_mod(xp_ref[...]), 0.0)
    h_next = jnp.where(i < pl.num_programs(1) - 1, norm_mod(xn_ref[...]), 0.0)
    h = jnp.concatenate([h_prev, norm_mod(x), h_next], axis=0).astype(BF16)

    half = t // ROW_SPLIT
    span = half + 2 * HALO
    cw, cb, fc = cw_ref[0], cb_ref[0], wup_ref.shape[2] // 2
    us = [jnp.dot(h[r * half:r * half + span, :], wup_ref[0], preferred_element_type=F32)
          for r in range(ROW_SPLIT)]
    ys = []
    for u in us:
        conv = (pltpu.roll(u, 1, 0) * cw[0:1, :] + u * cw[1:2, :]
                + pltpu.roll(u, span - 1, 0) * cw[2:3, :] + cb)
        conv = conv[HALO:HALO + half, :]
        gate, val = conv[:, :fc], conv[:, fc:]
        hid = (gate * jax.nn.sigmoid(gate)) * val
        ys.append(jnp.dot(hid.astype(BF16), wdn_ref[0], preferred_element_type=F32))
    y = jnp.concatenate(ys, axis=0)

    out = x + mod_ref[5:6, :] * y
    if final_norm:
        out = _rms(out) * gfin_ref[...]
    o_ref[...] = out


def _ffn_call(x, mods, g_ffn, w_up, conv_w, conv_b, w_down, g_final, *, layer, mod_row, final_norm):
    bsz, s, d = x.shape
    t = min(ROW_TILE, s)
    nt = s // t
    hpt = t // HALO
    n_halo = s // HALO
    nch, _, fc2 = w_up.shape[1:]
    kern = functools.partial(_ffn_kernel, final_norm=final_norm)
    once = pl.Buffered(1)
    return pl.pallas_call(
        kern,
        out_shape=jax.ShapeDtypeStruct((bsz, s, d), F32),
        grid=(bsz, nt),
        in_specs=[
            pl.BlockSpec((None, t, d), lambda b, i: (b, i, 0)),
            pl.BlockSpec((None, HALO, d), lambda b, i: (b, jnp.maximum(i * hpt - 1, 0), 0)),
            pl.BlockSpec((None, HALO, d),
                         lambda b, i: (b, jnp.minimum((i + 1) * hpt, n_halo - 1), 0)),
            pl.BlockSpec((None, None, 6, d), lambda b, i: (layer, mod_row(b), 0, 0)),
            pl.BlockSpec((None, 1, d), lambda b, i: (layer, 0, 0)),
            pl.BlockSpec((None, nch, d, fc2), lambda b, i: (layer, 0, 0, 0), pipeline_mode=once),
            pl.BlockSpec((None, nch, CONV_W, fc2), lambda b, i: (layer, 0, 0, 0)),
            pl.BlockSpec((None, nch, 1, fc2), lambda b, i: (layer, 0, 0, 0)),
            pl.BlockSpec((None, nch, fc2 // 2, d), lambda b, i: (layer, 0, 0, 0),
                         pipeline_mode=once),
            pl.BlockSpec((1, d), lambda b, i: (0, 0)),
        ],
        out_specs=pl.BlockSpec((None, t, d), lambda b, i: (b, i, 0)),
        compiler_params=_cparams(),
        name="conv_ffn",
    )(x, x, x, mods, g_ffn, w_up, conv_w, conv_b, w_down, g_final)


def _rope_tables(n_lat):
    rows = n_lat // GRID_W
    row = jnp.repeat(jnp.arange(rows, dtype=F32), GRID_W)
    col = jnp.tile(jnp.arange(GRID_W, dtype=F32), rows)
    inv_freq = ROPE_THETA ** (-jnp.arange(QUARTER, dtype=F32) / QUARTER)
    ang_r = row[:, None] * inv_freq[None, :]
    ang_c = col[:, None] * inv_freq[None, :]
    cos = jnp.concatenate([jnp.cos(ang_r), jnp.cos(ang_r), jnp.cos(ang_c), jnp.cos(ang_c)], axis=-1)
    sin = jnp.concatenate([jnp.sin(ang_r), jnp.sin(ang_r), jnp.sin(ang_c), jnp.sin(ang_c)], axis=-1)
    upper = (np.arange(HEAD_DIM) // QUARTER) % 2 == 1
    cos_t = cos.T
    sin_t = jnp.where(upper[None, :], sin, -sin).T
    rep = LANES // HEAD_DIM
    cos2, sin2 = jnp.tile(cos, (1, rep)), jnp.tile(sin, (1, rep))
    upper2 = np.tile(upper, rep)[None, :]
    sin_a = jnp.where(upper2, sin2, 0.0)
    sin_b = jnp.where(upper2, 0.0, -sin2)
    return cos2, sin_a, sin_b, cos_t, sin_t


def kernel(x, c, ctx, c_ctx, w_ada, b_ada, g_attn, w_in, g_qa, g_ka, lam_q1, lam_k1, lam_q2, lam_k2,
           g_sub, w_o, g_ffn, w_up, conv_w, conv_b, w_down, g_final):
    bsz, n_lat, d = x.shape
    ctx_len = ctx.shape[1]
    depth = w_ada.shape[0]
    a_heads, a_kv_heads, b_heads = d // 128, d // 512, d // 256
    a_q, a_kv = a_heads * HEAD_DIM, a_kv_heads * HEAD_DIM
    b_qk, b_v = b_heads * 2 * HEAD_DIM, b_heads * 2 * HEAD_DIM
    f = w_down.shape[1]
    assert a_kv == LANES
    assert all(n % min(ROW_TILE, n) == 0 and UNIT_COLS % min(ROW_TILE, n) == 0 for n in (ctx_len, n_lat))
    assert f % (FFN_CHUNKS * LANES) == 0

    pad = (-(bsz + 1)) % 8
    c_all = jnp.concatenate([c, c_ctx[None, :], jnp.zeros((pad, d), F32)], axis=0)
    mods = _ada_call(c_all, w_ada, b_ada).reshape(depth, bsz + 1 + pad, 6, d)
    row_x = lambda b: b
    row_c = lambda b: bsz

    o_ak, o_av, o_bq, o_bk, o_bv = np.cumsum([a_q, a_kv, a_kv, b_qk, b_qk])
    w_k = jnp.concatenate([w_in[:, :, o_ak:o_av], w_in[:, :, o_bk:o_bv]], axis=-1).astype(BF16)
    w_qvt = jnp.swapaxes(jnp.concatenate(
        [w_in[:, :, :o_ak], w_in[:, :, o_av:o_bq], w_in[:, :, o_bq:o_bk], w_in[:, :, o_bv:]],
        axis=-1), 1, 2).astype(BF16)
    w_ot = w_o.astype(BF16)

    fc = f // FFN_CHUNKS

    def chunk_cols(a):
        g_part = a[..., :f].reshape(a.shape[:-1] + (FFN_CHUNKS, fc))
        v_part = a[..., f:].reshape(a.shape[:-1] + (FFN_CHUNKS, fc))
        return jnp.concatenate([g_part, v_part], axis=-1)

    w_up_c = jnp.moveaxis(chunk_cols(w_up.astype(BF16)), 2, 1)
    conv_w_c = jnp.moveaxis(chunk_cols(conv_w), 2, 1)
    conv_b_c = chunk_cols(conv_b)[:, :, None, :]
    w_down_c = w_down.astype(BF16).reshape(depth, FFN_CHUNKS, fc, d)

    gq = jnp.broadcast_to(g_qa[:, :, None], (depth, HEAD_DIM, LANES))
    gk = jnp.tile(g_ka, (1, LANES // HEAD_DIM))[:, None, :]
    gsub = jnp.broadcast_to(g_sub[:, :, None], (depth, LANES, LANES))
    lamv = jnp.stack([lam_q1, lam_k1, lam_q2, lam_k2], axis=1)
    seg = np.arange(LANES) // HEAD_DIM
    pm = jnp.asarray((seg[:, None] == seg[None, :]) / HEAD_DIM, BF16)
    rope = _rope_tables(n_lat)
    heads = dict(a_heads=a_heads, a_kv_heads=a_kv_heads, b_heads=b_heads)

    xl, xc = x, ctx
    for l in range(depth):
        last = l == depth - 1
        lam_init = 0.8 - 0.6 * math.exp(-0.3 * l)
        qc, kc, vc = _qkv_call(xc, mods, g_attn[:, None, :], w_k, w_qvt, gq, gk, pm, None,
                               layer=l, mod_row=row_c, **heads)
        qx, kx, vx = _qkv_call(xl, mods, g_attn[:, None, :], w_k, w_qvt, gq, gk, pm, rope,
                               layer=l, mod_row=row_x, **heads)
        xl = _attn_call(qx, [kc, kx], [vc, vx], xl, mods, w_ot, gsub, lamv, layer=l, mod_row=row_x,
                        lam_init=lam_init, **heads)
        xl = _ffn_call(xl, mods, g_ffn[:, None, :], w_up_c, conv_w_c, conv_b_c, w_down_c,
                       g_final[None, :], layer=l, mod_row=row_x, final_norm=last)
        if not last:
            xc = _attn_call(qc, [kc], [vc], xc, mods, w_ot, gsub, lamv, layer=l, mod_row=row_c,
                            lam_init=lam_init, **heads)
            xc = _ffn_call(xc, mods, g_ffn[:, None, :], w_up_c, conv_w_c, conv_b_c, w_down_c,
                           g_final[None, :], layer=l, mod_row=row_c, final_norm=False)
    return xl
```

```python
import functools
import math

import jax
import jax.numpy as jnp
import numpy as np
from jax import lax
from jax.experimental import pallas as pl
from jax.experimental.pallas import tpu as pltpu

F32 = jnp.float32
BF16 = jnp.bfloat16

HEAD_DIM = 64
LANES = 128
GRID_W = 64
ROPE_THETA = 10000.0
EPS = 1e-6
CONV_W = 3
QUARTER = HEAD_DIM // 4

ROW_TILE = 512
UNIT_COLS = 512
KEY_CHUNK = 256
V_ROWS = LANES + 16
FFN_CHUNKS = 1
ROW_SPLIT = 2
HALO = 8
VMEM_LIMIT = 56 * 1024 * 1024


def _cparams():
    return pltpu.CompilerParams(
        dimension_semantics=("parallel", "arbitrary"),
        vmem_limit_bytes=VMEM_LIMIT,
    )


def _rms(x):
    return x * lax.rsqrt(jnp.mean(x * x, axis=-1, keepdims=True) + EPS)


def _lane_tile(a, width):
    return jnp.concatenate([a] * (width // LANES), axis=1)


def _ada_kernel(c_ref, w_ref, b_ref, o_ref):
    c = c_ref[...]
    s = c * jax.nn.sigmoid(c)
    o_ref[...] = jnp.dot(s.astype(BF16), w_ref[...].astype(BF16),
                         preferred_element_type=F32) + b_ref[...]


def _ada_call(c_all, w_ada, b_ada):
    depth, d, d6 = w_ada.shape
    rows = c_all.shape[0]
    tn = d6 // 4
    return pl.pallas_call(
        _ada_kernel,
        out_shape=jax.ShapeDtypeStruct((depth, rows, d6), F32),
        grid=(depth, d6 // tn),
        in_specs=[
            pl.BlockSpec((rows, d), lambda l, j: (0, 0)),
            pl.BlockSpec((None, d, tn), lambda l, j: (l, 0, j)),
            pl.BlockSpec((None, 1, tn), lambda l, j: (l, 0, j)),
        ],
        out_specs=pl.BlockSpec((None, rows, tn), lambda l, j: (l, 0, j)),
        compiler_params=_cparams(),
        name="adaln",
    )(c_all, w_ada, b_ada.reshape(depth, 1, d6))


def _qkv_kernel(*refs, a_heads, a_kv_heads, b_heads, use_rope):
    if use_rope:
        (x_ref, mod_ref, g_ref, wk_ref, wqv_ref, gq_ref, gk_ref, pm_ref,
         cos_ref, sa_ref, sb_ref, cost_ref, sint_ref, qt_ref, k_ref, vt_ref) = refs
    else:
        (x_ref, mod_ref, g_ref, wk_ref, wqv_ref, gq_ref, gk_ref, pm_ref,
         qt_ref, k_ref, vt_ref) = refs
    t = x_ref.shape[0]
    x = x_ref[...]
    h = (_rms(x) * g_ref[...]) * (1.0 + mod_ref[1:2, :]) + mod_ref[0:1, :]
    half = t // ROW_SPLIT
    hb = h.astype(BF16)
    proj = []
    for r in range(ROW_SPLIT):
        h_r = h[r * half:(r + 1) * half, :]
        proj.append((jnp.dot(hb[r * half:(r + 1) * half, :], wk_ref[...], preferred_element_type=F32),
                     jnp.dot(wqv_ref[...], h_r.T.astype(BF16), preferred_element_type=F32)))

    q_scale = HEAD_DIM ** -0.5 * math.log2(math.e)
    zero = jnp.zeros((HEAD_DIM, half), BF16)
    ones = jnp.ones((V_ROWS - LANES, half), BF16)
    gq = _lane_tile(gq_ref[...], half)
    group = a_heads // a_kv_heads

    for r, (pk, pt) in enumerate(proj):
        tok = slice(r * half, (r + 1) * half)

        def rope_k(z):
            if not use_rope:
                return z
            return (z * cos_ref[tok, :] + pltpu.roll(z, QUARTER, 1) * sa_ref[tok, :]
                    + pltpu.roll(z, LANES - QUARTER, 1) * sb_ref[tok, :])

        ka = pk[:, 0:LANES]
        sq = ka * ka
        hi = sq.astype(BF16)
        lo = (sq - hi.astype(F32)).astype(BF16)
        ms = (jnp.dot(hi, pm_ref[...], preferred_element_type=F32)
              + jnp.dot(lo, pm_ref[...], preferred_element_type=F32))
        k_ref[0, tok, :] = rope_k(ka * lax.rsqrt(ms + EPS) * gk_ref[...]).astype(BF16)
        for hh in range(b_heads):
            k_ref[1 + hh, tok, :] = rope_k(pk[:, (1 + hh) * LANES:(2 + hh) * LANES]).astype(BF16)

        def rope_q(z):
            if not use_rope:
                return z
            swap = jnp.concatenate([z[QUARTER:2 * QUARTER], z[0:QUARTER],
                                    z[3 * QUARTER:4 * QUARTER], z[2 * QUARTER:3 * QUARTER]], axis=0)
            return z * cost_ref[:, tok] + swap * sint_ref[:, tok]

        def put_q(blk, lane_half, z):
            qt_ref[blk, lane_half * HEAD_DIM:(lane_half + 1) * HEAD_DIM, tok] = (z * q_scale).astype(BF16)
            qt_ref[blk, (1 - lane_half) * HEAD_DIM:(2 - lane_half) * HEAD_DIM, tok] = zero

        for hh in range(a_heads):
            z = pt[hh * HEAD_DIM:(hh + 1) * HEAD_DIM, :]
            z = z * lax.rsqrt(jnp.mean(z * z, axis=0, keepdims=True) + EPS) * gq
            put_q(hh, hh // group, rope_q(z))
        off = a_heads * HEAD_DIM
        vt_ref[0, 0:LANES, tok] = pt[off:off + LANES, :].astype(BF16)
        vt_ref[0, LANES:V_ROWS, tok] = ones
        off += LANES
        for hh in range(b_heads):
            for j in range(2):
                put_q(a_heads + 2 * hh + j, j, rope_q(pt[off:off + HEAD_DIM, :]))
                off += HEAD_DIM
        for hh in range(b_heads):
            vt_ref[1 + hh, 0:LANES, tok] = pt[off:off + LANES, :].astype(BF16)
            vt_ref[1 + hh, LANES:V_ROWS, tok] = ones
            off += LANES


def _qkv_call(x, mods, g_attn, w_k, w_qvt, gq, gk, pm, rope, *, layer, mod_row, a_heads,
              a_kv_heads, b_heads):
    bsz, s, d = x.shape
    t = min(ROW_TILE, s)
    nt = s // t
    n_kv = 1 + b_heads
    n_qblk = a_heads + 2 * b_heads
    use_rope = rope is not None
    kern = functools.partial(_qkv_kernel, a_heads=a_heads, a_kv_heads=a_kv_heads, b_heads=b_heads,
                             use_rope=use_rope)
    in_specs = [
        pl.BlockSpec((None, t, d), lambda b, i: (b, i, 0)),
        pl.BlockSpec((None, None, 6, d), lambda b, i: (layer, mod_row(b), 0, 0)),
        pl.BlockSpec((None, 1, d), lambda b, i: (layer, 0, 0)),
        pl.BlockSpec((None,) + w_k.shape[1:], lambda b, i: (layer, 0, 0)),
        pl.BlockSpec((None,) + w_qvt.shape[1:], lambda b, i: (layer, 0, 0)),
        pl.BlockSpec((None, HEAD_DIM, LANES), lambda b, i: (layer, 0, 0)),
        pl.BlockSpec((None, 1, LANES), lambda b, i: (layer, 0, 0)),
        pl.BlockSpec((LANES, LANES), lambda b, i: (0, 0)),
    ]
    args = [x, mods, g_attn, w_k, w_qvt, gq, gk, pm]
    if use_rope:
        in_specs += [pl.BlockSpec((t, LANES), lambda b, i: (i, 0))] * 3
        in_specs += [pl.BlockSpec((HEAD_DIM, t), lambda b, i: (0, i))] * 2
        args += list(rope)
    return pl.pallas_call(
        kern,
        out_shape=(
            jax.ShapeDtypeStruct((bsz, nt, n_qblk, LANES, t), BF16),
            jax.ShapeDtypeStruct((bsz, n_kv, s, LANES), BF16),
            jax.ShapeDtypeStruct((bsz, n_kv, V_ROWS, s), BF16),
        ),
        grid=(bsz, nt),
        in_specs=in_specs,
        out_specs=(
            pl.BlockSpec((None, None, n_qblk, LANES, t), lambda b, i: (b, i, 0, 0, 0)),
            pl.BlockSpec((None, n_kv, t, LANES), lambda b, i: (b, 0, i, 0)),
            pl.BlockSpec((None, n_kv, V_ROWS, t), lambda b, i: (b, 0, 0, i)),
        ),
        compiler_params=_cparams(),
        name="qkv_proj_x" if use_rope else "qkv_proj_c",
    )(*args)


def _attn_kernel(*refs, n_seg, a_heads, a_kv_heads, b_heads, lam_init):
    qt_ref = refs[0]
    k_refs = refs[1:1 + n_seg]
    vt_refs = refs[1 + n_seg:1 + 2 * n_seg]
    x_ref, mod_ref, wot_ref, gsub_ref, lam_ref, o_ref, ot_sc, mrg_sc, sa_sc, sb_sc = refs[1 + 2 * n_seg:]
    t = x_ref.shape[0]
    qpu = UNIT_COLS // t
    n_units = qt_ref.shape[0] // qpu

    chunks, row = [], 0
    for seg, k_ref in enumerate(k_refs):
        n = k_ref.shape[1]
        ck = min(n, KEY_CHUNK)
        for c0 in range(0, n, ck):
            chunks.append((seg, c0, ck, row + c0))
        row += n

    def kv_group(u):
        return jnp.maximum(lax.shift_right_arithmetic(u * qpu - a_heads, 1) + 1, 0)

    def block_out(blk):
        return ot_sc.at[blk // qpu, :, (blk % qpu) * t:(blk % qpu + 1) * t]

    def step(u_next, s_next, u_cur, s_cur, m_cur):
        if u_next is not None:
            kv_n = kv_group(u_next)
            qt = jnp.concatenate([qt_ref[qpu * u_next + j] for j in range(qpu)], axis=1)
        if u_cur is not None:
            kv_c = kv_group(u_cur)
        m_next = acc = None
        for seg, c0, ck, r0 in chunks:
            if u_next is not None:
                s = jnp.dot(k_refs[seg][kv_n, c0:c0 + ck, :], qt,
                            preferred_element_type=F32).astype(BF16)
                s_next[r0:r0 + ck, :] = s
                m_c = jnp.max(s, axis=0, keepdims=True)
                m_next = m_c if m_next is None else jnp.maximum(m_next, m_c)
            if u_cur is not None:
                p = jnp.exp2(s_cur[r0:r0 + ck, :] - m_cur)
                part = jnp.dot(vt_refs[seg][kv_c, :, c0:c0 + ck], p, preferred_element_type=F32)
                acc = part if acc is None else acc + part
        if u_cur is not None:
            ot_sc[u_cur] = acc[0:LANES, :] * (1.0 / acc[LANES:LANES + 1, :])
        return m_next

    def unit_pair(i, m_a):
        m_b = step(2 * i + 1, sb_sc, 2 * i, sa_sc, m_a)
        return step(2 * i + 2, sa_sc, 2 * i + 1, sb_sc, m_b)

    assert n_units % 2 == 0
    m_a = step(0, sa_sc, None, None, None)
    m_a = lax.fori_loop(0, n_units // 2 - 1, unit_pair, m_a)
    m_b = step(n_units - 1, sb_sc, n_units - 2, sa_sc, m_a)
    step(None, None, n_units - 1, sb_sc, m_b)

    group = a_heads // a_kv_heads
    for hh in range(a_heads):
        kvh = hh // group
        blk = block_out(hh)[kvh * HEAD_DIM:(kvh + 1) * HEAD_DIM, :]
        mrg_sc[hh * HEAD_DIM:(hh + 1) * HEAD_DIM, :] = blk.astype(BF16)

    lv = lam_ref[...]
    lam = (jnp.exp(jnp.sum(lv[0:1, :] * lv[1:2, :], axis=-1, keepdims=True))
           - jnp.exp(jnp.sum(lv[2:3, :] * lv[3:4, :], axis=-1, keepdims=True)) + lam_init)
    gsub = _lane_tile(gsub_ref[...], t)
    a_w = a_heads * HEAD_DIM
    for hh in range(b_heads):
        diff = block_out(a_heads + 2 * hh)[...] - lam * block_out(a_heads + 2 * hh + 1)[...]
        sub = diff * lax.rsqrt(jnp.mean(diff * diff, axis=0, keepdims=True) + EPS) * gsub
        mrg_sc[a_w + hh * LANES:a_w + (hh + 1) * LANES, :] = (sub * (1.0 - lam_init)).astype(BF16)

    y = lax.dot_general(mrg_sc[...], wot_ref[...], (((0,), (0,)), ((), ())),
                        preferred_element_type=F32)
    o_ref[...] = x_ref[...] + mod_ref[2:3, :] * y


def _attn_call(qt, ks, vts, x, mods, w_ot, gsub, lamv, *, layer, mod_row, a_heads, a_kv_heads,
               b_heads, lam_init):
    bsz, s, d = x.shape
    t = min(ROW_TILE, s)
    nt = s // t
    n_seg = len(ks)
    assert UNIT_COLS % t == 0 and a_heads % (UNIT_COLS // t) == 0
    n_units = qt.shape[2] * t // UNIT_COLS
    n_keys = sum(k.shape[2] for k in ks)
    kern = functools.partial(_attn_kernel, n_seg=n_seg, a_heads=a_heads, a_kv_heads=a_kv_heads,
                             b_heads=b_heads, lam_init=lam_init)
    once = pl.Buffered(1)
    in_specs = [pl.BlockSpec((None, None) + qt.shape[2:], lambda b, i: (b, i, 0, 0, 0))]
    in_specs += [pl.BlockSpec((None,) + kv.shape[1:], lambda b, i: (b, 0, 0, 0)) for kv in ks + vts]
    in_specs += [
        pl.BlockSpec((None, t, d), lambda b, i: (b, i, 0)),
        pl.BlockSpec((None, None, 6, d), lambda b, i: (layer, mod_row(b), 0, 0)),
        pl.BlockSpec((None, d, d), lambda b, i: (layer, 0, 0), pipeline_mode=once),
        pl.BlockSpec((None, LANES, LANES), lambda b, i: (layer, 0, 0)),
        pl.BlockSpec((None, 4, HEAD_DIM), lambda b, i: (layer, 0, 0)),
    ]
    return pl.pallas_call(
        kern,
        out_shape=jax.ShapeDtypeStruct((bsz, s, d), F32),
        grid=(bsz, nt),
        in_specs=in_specs,
        out_specs=pl.BlockSpec((None, t, d), lambda b, i: (b, i, 0)),
        scratch_shapes=[
            pltpu.VMEM((n_units, LANES, UNIT_COLS), F32),
            pltpu.VMEM((d, t), BF16),
        ] + [pltpu.VMEM((n_keys, UNIT_COLS), BF16)] * 2,
        compiler_params=_cparams(),
        name="attention_x" if n_seg == 2 else "attention_c",
    )(qt, *ks, *vts, x, mods, w_ot, gsub, lamv)


def _ffn_kernel(x_ref, xp_ref, xn_ref, mod_ref, g_ref, wup_ref, cw_ref, cb_ref, wdn_ref, gfin_ref,
                o_ref, *, final_norm):
    t = x_ref.shape[0]
    i = pl.program_id(1)

    def norm_mod(x):
        return (_rms(x) * g_ref[...]) * (1.0 + mod_ref[4:5, :]) + mod_ref[3:4, :]

    x = x_ref[...]
    h_prev = jnp.where(i > 0, norm_mod(xp_ref[...]), 0.0)
    h_next = jnp.where(i < pl.num_programs(1) - 1, norm_mod(xn_ref[...]), 0.0)
    h = jnp.concatenate([h_prev, norm_mod(x), h_next], axis=0).astype(BF16)

    half = t // ROW_SPLIT
    span = half + 2 * HALO
    cw, cb, fc = cw_ref[0], cb_ref[0], wup_ref.shape[2] // 2
    us = [jnp.dot(h[r * half:r * half + span, :], wup_ref[0], preferred_element_type=F32)
          for r in range(ROW_SPLIT)]
    ys = []
    for u in us:
        conv = (pltpu.roll(u, 1, 0) * cw[0:1, :] + u * cw[1:2, :]
                + pltpu.roll(u, span - 1, 0) * cw[2:3, :] + cb)
        conv = conv[HALO:HALO + half, :]
        gate, val = conv[:, :fc], conv[:, fc:]
        hid = (gate * jax.nn.sigmoid(gate)) * val
        ys.append(jnp.dot(hid.astype(BF16), wdn_ref[0], preferred_element_type=F32))
    y = jnp.concatenate(ys, axis=0)

    out = x + mod_ref[5:6, :] * y
    if final_norm:
        out = _rms(out) * gfin_ref[...]
    o_ref[...] = out


def _ffn_call(x, mods, g_ffn, w_up, conv_w, conv_b, w_down, g_final, *, layer, mod_row, final_norm):
    bsz, s, d = x.shape
    t = min(ROW_TILE, s)
    nt = s // t
    hpt = t // HALO
    n_halo = s // HALO
    nch, _, fc2 = w_up.shape[1:]
    kern = functools.partial(_ffn_kernel, final_norm=final_norm)
    once = pl.Buffered(1)
    return pl.pallas_call(
        kern,
        out_shape=jax.ShapeDtypeStruct((bsz, s, d), F32),
        grid=(bsz, nt),
        in_specs=[
            pl.BlockSpec((None, t, d), lambda b, i: (b, i, 0)),
            pl.BlockSpec((None, HALO, d), lambda b, i: (b, jnp.maximum(i * hpt - 1, 0), 0)),
            pl.BlockSpec((None, HALO, d),
                         lambda b, i: (b, jnp.minimum((i + 1) * hpt, n_halo - 1), 0)),
            pl.BlockSpec((None, None, 6, d), lambda b, i: (layer, mod_row(b), 0, 0)),
            pl.BlockSpec((None, 1, d), lambda b, i: (layer, 0, 0)),
            pl.BlockSpec((None, nch, d, fc2), lambda b, i: (layer, 0, 0, 0), pipeline_mode=once),
            pl.BlockSpec((None, nch, CONV_W, fc2), lambda b, i: (layer, 0, 0, 0)),
            pl.BlockSpec((None, nch, 1, fc2), lambda b, i: (layer, 0, 0, 0)),
            pl.BlockSpec((None, nch, fc2 // 2, d), lambda b, i: (layer, 0, 0, 0),
                         pipeline_mode=once),
            pl.BlockSpec((1, d), lambda b, i: (0, 0)),
        ],
        out_specs=pl.BlockSpec((None, t, d), lambda b, i: (b, i, 0)),
        compiler_params=_cparams(),
        name="conv_ffn",
    )(x, x, x, mods, g_ffn, w_up, conv_w, conv_b, w_down, g_final)


def _rope_tables(n_lat):
    rows = n_lat // GRID_W
    row = jnp.repeat(jnp.arange(rows, dtype=F32), GRID_W)
    col = jnp.tile(jnp.arange(GRID_W, dtype=F32), rows)
    inv_freq = ROPE_THETA ** (-jnp.arange(QUARTER, dtype=F32) / QUARTER)
    ang_r = row[:, None] * inv_freq[None, :]
    ang_c = col[:, None] * inv_freq[None, :]
    cos = jnp.concatenate([jnp.cos(ang_r), jnp.cos(ang_r), jnp.cos(ang_c), jnp.cos(ang_c)], axis=-1)
    sin = jnp.concatenate([jnp.sin(ang_r), jnp.sin(ang_r), jnp.sin(ang_c), jnp.sin(ang_c)], axis=-1)
    upper = (np.arange(HEAD_DIM) // QUARTER) % 2 == 1
    cos_t = cos.T
    sin_t = jnp.where(upper[None, :], sin, -sin).T
    rep = LANES // HEAD_DIM
    cos2, sin2 = jnp.tile(cos, (1, rep)), jnp.tile(sin, (1, rep))
    upper2 = np.tile(upper, rep)[None, :]
    sin_a = jnp.where(upper2, sin2, 0.0)
    sin_b = jnp.where(upper2, 0.0, -sin2)
    return cos2, sin_a, sin_b, cos_t, sin_t


def kernel(x, c, ctx, c_ctx, w_ada, b_ada, g_attn, w_in, g_qa, g_ka, lam_q1, lam_k1, lam_q2, lam_k2,
           g_sub, w_o, g_ffn, w_up, conv_w, conv_b, w_down, g_final):
    bsz, n_lat, d = x.shape
    ctx_len = ctx.shape[1]
    depth = w_ada.shape[0]
    a_heads, a_kv_heads, b_heads = d // 128, d // 512, d // 256
    a_q, a_kv = a_heads * HEAD_DIM, a_kv_heads * HEAD_DIM
    b_qk, b_v = b_heads * 2 * HEAD_DIM, b_heads * 2 * HEAD_DIM
    f = w_down.shape[1]
    assert a_kv == LANES
    assert all(n % min(ROW_TILE, n) == 0 and UNIT_COLS % min(ROW_TILE, n) == 0 for n in (ctx_len, n_lat))
    assert f % (FFN_CHUNKS * LANES) == 0

    pad = (-(bsz + 1)) % 8
    c_all = jnp.concatenate([c, c_ctx[None, :], jnp.zeros((pad, d), F32)], axis=0)
    mods = _ada_call(c_all, w_ada, b_ada).reshape(depth, bsz + 1 + pad, 6, d)
    row_x = lambda b: b
    row_c = lambda b: bsz

    o_ak, o_av, o_bq, o_bk, o_bv = np.cumsum([a_q, a_kv, a_kv, b_qk, b_qk])
    w_k = jnp.concatenate([w_in[:, :, o_ak:o_av], w_in[:, :, o_bk:o_bv]], axis=-1).astype(BF16)
    w_qvt = jnp.swapaxes(jnp.concatenate(
        [w_in[:, :, :o_ak], w_in[:, :, o_av:o_bq], w_in[:, :, o_bq:o_bk], w_in[:, :, o_bv:]],
        axis=-1), 1, 2).astype(BF16)
    w_ot = w_o.astype(BF16)

    fc = f // FFN_CHUNKS

    def chunk_cols(a):
        g_part = a[..., :f].reshape(a.shape[:-1] + (FFN_CHUNKS, fc))
        v_part = a[..., f:].reshape(a.shape[:-1] + (FFN_CHUNKS, fc))
        return jnp.concatenate([g_part, v_part], axis=-1)

    w_up_c = jnp.moveaxis(chunk_cols(w_up.astype(BF16)), 2, 1)
    conv_w_c = jnp.moveaxis(chunk_cols(conv_w), 2, 1)
    conv_b_c = chunk_cols(conv_b)[:, :, None, :]
    w_down_c = w_down.astype(BF16).reshape(depth, FFN_CHUNKS, fc, d)

    gq = jnp.broadcast_to(g_qa[:, :, None], (depth, HEAD_DIM, LANES))
    gk = jnp.tile(g_ka, (1, LANES // HEAD_DIM))[:, None, :]
    gsub = jnp.broadcast_to(g_sub[:, :, None], (depth, LANES, LANES))
    lamv = jnp.stack([lam_q1, lam_k1, lam_q2, lam_k2], axis=1)
    seg = np.arange(LANES) // HEAD_DIM
    pm = jnp.asarray((seg[:, None] == seg[None, :]) / HEAD_DIM, BF16)
    rope = _rope_tables(n_lat)
    heads = dict(a_heads=a_heads, a_kv_heads=a_kv_heads, b_heads=b_heads)

    xl, xc = x, ctx
    for l in range(depth):
        last = l == depth - 1
        lam_init = 0.8 - 0.6 * math.exp(-0.3 * l)
        qc, kc, vc = _qkv_call(xc, mods, g_attn[:, None, :], w_k, w_qvt, gq, gk, pm, None,
                               layer=l, mod_row=row_c, **heads)
        qx, kx, vx = _qkv_call(xl, mods, g_attn[:, None, :], w_k, w_qvt, gq, gk, pm, rope,
                               layer=l, mod_row=row_x, **heads)
        xl = _attn_call(qx, [kc, kx], [vc, vx], xl, mods, w_ot, gsub, lamv, layer=l, mod_row=row_x,
                        lam_init=lam_init, **heads)
        xl = _ffn_call(xl, mods, g_ffn[:, None, :], w_up_c, conv_w_c, conv_b_c, w_down_c,
                       g_final[None, :], layer=l, mod_row=row_x, final_norm=last)
        if not last:
            xc = _attn_call(qc, [kc], [vc], xc, mods, w_ot, gsub, lamv, layer=l, mod_row=row_c,
                            lam_init=lam_init, **heads)
            xc = _ffn_call(xc, mods, g_ffn[:, None, :], w_up_c, conv_w_c, conv_b_c, w_down_c,
                           g_final[None, :], layer=l, mod_row=row_c, final_norm=False)
    return xl
```
